```python
import math
import jax, jax.numpy as jnp
from jax import lax
import numpy as np

D_MODEL = 1024
BATCH = 16
SEQ = 256
DEPTH = 2
DEC_BATCH = 8
DEC_SEQ = 2048
PAST_LEN = 512

GRID_W = 64
CONV_CH = D_MODEL
CONV_W = 31
CONV_PAD = CONV_W // 2
RET_HEADS = 4
RET_DK = 256
RET_DV = 512
RET_QK_W = RET_HEADS * RET_DK
RET_V_W = RET_HEADS * RET_DV
CHUNK = 128
MLP_W = 4 * D_MODEL
PROJ_W = 2 * CONV_CH + 2 * RET_QK_W + 2 * RET_V_W + 2 * D_MODEL
ROPE_BASE = 10000.0
EPS = 1e-6

kernel_name = "hybrid_conformer_retention_dit_step"


def _rmsnorm(x, g):
    xf = x.astype(jnp.float32)
    y = xf * lax.rsqrt(jnp.mean(xf * xf, axis=-1, keepdims=True) + EPS)
    return (y * g.astype(jnp.float32)).astype(x.dtype)


def _head_norm(y):
    yf = y.astype(jnp.float32)
    mu = jnp.mean(yf, axis=-1, keepdims=True)
    var = jnp.mean(jnp.square(yf - mu), axis=-1, keepdims=True)
    return ((yf - mu) * lax.rsqrt(var + EPS)).astype(y.dtype)


def _rope_2d(x):
    L = x.shape[1]
    t = jnp.arange(L)
    row = (t // GRID_W).astype(jnp.float32)
    col = (t % GRID_W).astype(jnp.float32)
    nf = RET_DK // 4
    inv = ROPE_BASE ** (-jnp.arange(nf, dtype=jnp.float32) / nf)
    ang = jnp.concatenate([row[:, None] * inv, col[:, None] * inv], axis=-1)[None, :, None, :]
    cos = jnp.cos(ang).astype(x.dtype)
    sin = jnp.sin(ang).astype(x.dtype)
    x1, x2 = jnp.split(x, 2, axis=-1)
    return jnp.concatenate([x1 * cos - x2 * sin, x2 * cos + x1 * sin], axis=-1)


def _dwconv(u, w, b):
    out = lax.conv_general_dilated(
        u, w[:, None, :].astype(u.dtype), window_strides=(1,), padding=[(CONV_PAD, CONV_PAD)],
        dimension_numbers=("NWC", "WIO", "NWC"), feature_group_count=u.shape[-1])
    return out + b.astype(u.dtype)


def _retention_scan(q, k, v, log_gamma, s0):
    B, L, H, DK = q.shape
    DV = v.shape[-1]
    n = L // CHUNK
    dt = q.dtype
    lg = log_gamma.astype(jnp.float32)
    idx = jnp.arange(CHUNK, dtype=jnp.float32)
    diff = idx[:, None] - idx[None, :]
    dmat = jnp.where(diff[None] >= 0, jnp.exp(jnp.maximum(diff, 0.0)[None] * lg[:, None, None]), 0.0).astype(dt)
    q_dec = jnp.exp((idx + 1.0)[:, None] * lg[None, :]).astype(dt)
    k_dec = jnp.exp((CHUNK - 1.0 - idx)[:, None] * lg[None, :]).astype(dt)
    s_dec = jnp.exp(CHUNK * lg).astype(dt)

    def chunks(t):
        return jnp.moveaxis(t.reshape(B, n, CHUNK, H, t.shape[-1]), 1, 0)

    def step(s, inp):
        qc, kc, vc = inp
        scores = jnp.einsum("bihd,bjhd->bhij", qc, kc) * dmat
        y = (jnp.einsum("bhij,bjhe->bihe", scores, vc)
             + jnp.einsum("bihd,bhde->bihe", qc, s) * q_dec[None, :, :, None])
        s = s * s_dec[None, :, None, None] + jnp.einsum("bjhd,bjhe->bhde", kc * k_dec[None, :, :, None], vc)
        return s, y

    s_fin, ys = lax.scan(step, s0.astype(dt), (chunks(q), chunks(k), chunks(v)))
    y = jnp.moveaxis(ys, 0, 1).reshape(B, L, H, DV)
    return y, s_fin


def _layer(x, mod, s0, grid, norm1_g, norm2_g, w_in, conv_dw, conv_b, conv_norm_g, w_conv_out,
           decay_logit, w_ret_out, w_out, w_mlp1, w_mlp2):
    B, L, _ = x.shape
    sh1, sc1, g1, sh2, sc2, g2 = jnp.split(mod[:, None, :].astype(x.dtype), 6, axis=-1)
    h = _rmsnorm(x, norm1_g) * (1.0 + sc1) + sh1
    split_idx = [2 * CONV_CH, 2 * CONV_CH + RET_QK_W, 2 * CONV_CH + 2 * RET_QK_W,
                 2 * CONV_CH + 2 * RET_QK_W + RET_V_W, 2 * CONV_CH + 2 * RET_QK_W + 2 * RET_V_W]
    u, q, k, v, rg, bg = jnp.split(h @ w_in, split_idx, axis=-1)

    a = u[..., :CONV_CH] * jax.nn.sigmoid(u[..., CONV_CH:])
    if grid:
        rows = L // GRID_W
        a = _dwconv(a.reshape(B * rows, GRID_W, CONV_CH), conv_dw, conv_b).reshape(B, L, CONV_CH)
    else:
        a = _dwconv(a, conv_dw, conv_b)
    a = jax.nn.silu(_rmsnorm(a, conv_norm_g)) @ w_conv_out

    q = q.reshape(B, L, RET_HEADS, RET_DK)
    k = k.reshape(B, L, RET_HEADS, RET_DK) * (RET_DK ** -0.5)
    v = v.reshape(B, L, RET_HEADS, RET_DV)
    if grid:
        q = _rope_2d(q)
        k = _rope_2d(k)
    log_g = jax.nn.log_sigmoid(decay_logit.astype(jnp.float32))
    y_f, s_f = _retention_scan(q, k, v, log_g[0], s0[:, 0])
    y_b, s_b = _retention_scan(q[:, ::-1], k[:, ::-1], v[:, ::-1], log_g[1], s0[:, 1])
    y = _head_norm(y_f + y_b[:, ::-1]).reshape(B, L, RET_V_W)
    b = (jax.nn.silu(rg) * y) @ w_ret_out

    ga, gb = jnp.split(jax.nn.sigmoid(bg), 2, axis=-1)
    x = x + g1 * ((ga * a + gb * b) @ w_out)

    h2 = _rmsnorm(x, norm2_g) * (1.0 + sc2) + sh2
    x = x + g2 * (jnp.square(jax.nn.relu(h2 @ w_mlp1)) @ w_mlp2)
    return x, jnp.stack([s_f, s_b], axis=1)


def setup_inputs(seed: int = 0) -> dict:
    key = jax.random.key(seed)
    ks = jax.random.split(key, 24)
    f32 = jnp.float32

    def nrm(k, shape, scale):
        return jax.random.normal(k, shape, f32) * scale

    base_logit = jnp.log(2.0 ** (5.0 + jnp.arange(RET_HEADS, dtype=f32)) - 1.0)
    return {
        "x_prompt": nrm(ks[0], (BATCH, SEQ, D_MODEL), 1.0),
        "x_sample": nrm(ks[1], (DEC_BATCH, DEC_SEQ, D_MODEL), 1.0),
        "c": nrm(ks[2], (DEC_BATCH, D_MODEL), 1.0),
        "state_ret": nrm(ks[3], (DEC_BATCH, DEPTH, 2, RET_HEADS, RET_DK, RET_DV), 0.05),
        "c_ctx": nrm(ks[4], (D_MODEL,), 1.0),
        "norm1_g": 1.0 + nrm(ks[5], (DEPTH, D_MODEL), 0.02),
        "norm2_g": 1.0 + nrm(ks[6], (DEPTH, D_MODEL), 0.02),
        "w_mod": nrm(ks[7], (DEPTH, D_MODEL, 6 * D_MODEL), D_MODEL ** -0.5),
        "b_mod": nrm(ks[8], (DEPTH, 6 * D_MODEL), 0.02),
        "w_in": nrm(ks[9], (DEPTH, D_MODEL, PROJ_W), D_MODEL ** -0.5),
        "conv_dw": nrm(ks[10], (DEPTH, CONV_W, CONV_CH), CONV_W ** -0.5),
        "conv_b": nrm(ks[11], (DEPTH, CONV_CH), 0.02),
        "conv_norm_g": 1.0 + nrm(ks[12], (DEPTH, CONV_CH), 0.02),
        "w_conv_out": nrm(ks[13], (DEPTH, CONV_CH, D_MODEL), CONV_CH ** -0.5),
        "ret_decay_logit": base_logit[None, None, :] + nrm(ks[14], (DEPTH, 2, RET_HEADS), 0.1),
        "w_ret_out": nrm(ks[15], (DEPTH, RET_V_W, D_MODEL), RET_V_W ** -0.5),
        "w_out": nrm(ks[16], (DEPTH, D_MODEL, D_MODEL), D_MODEL ** -0.5),
        "w_mlp1": nrm(ks[17], (DEPTH, D_MODEL, MLP_W), D_MODEL ** -0.5),
        "w_mlp2": nrm(ks[18], (DEPTH, MLP_W, D_MODEL), MLP_W ** -0.5),
        "final_norm_g": 1.0 + nrm(ks[19], (D_MODEL,), 0.02),
    }


def reference(x_prompt, x_sample, c, state_ret, c_ctx, norm1_g, norm2_g, w_mod, b_mod, w_in,
              conv_dw, conv_b, conv_norm_g, w_conv_out, ret_decay_logit, w_ret_out, w_out,
              w_mlp1, w_mlp2, final_norm_g):
    xp = x_prompt
    xs = x_sample
    cond_ctx = jax.nn.silu(c_ctx)[None, :]
    cond_lat = jax.nn.silu(c)
    zero_state = jnp.zeros((xp.shape[0], 2, RET_HEADS, RET_DK, RET_DV), xp.dtype)
    new_states = []
    for l in range(DEPTH):
        lw = (norm1_g[l], norm2_g[l], w_in[l], conv_dw[l], conv_b[l], conv_norm_g[l], w_conv_out[l],
              ret_decay_logit[l], w_ret_out[l], w_out[l], w_mlp1[l], w_mlp2[l])
        mod_ctx = cond_ctx @ w_mod[l] + b_mod[l]
        xp, st = _layer(xp, mod_ctx, zero_state, False, *lw)
        new_states.append(st)
        mod_lat = cond_lat @ w_mod[l] + b_mod[l]
        xs, _ = _layer(xs, mod_lat, state_ret[:, l], True, *lw)
    new_state_ret = jnp.stack(new_states, axis=1)
    y_prompt = _rmsnorm(xp, final_norm_g)
    y_sample = _rmsnorm(xs, final_norm_g)
    return (y_prompt, y_sample, new_state_ret)
```

```python
import functools
import math

import jax
import jax.numpy as jnp
from jax import lax
from jax.experimental import pallas as pl
from jax.experimental.pallas import tpu as pltpu

D_MODEL = 1024
GRID_W = 64
CONV_CH = D_MODEL
CONV_W = 31
CONV_PAD = CONV_W // 2
RET_HEADS = 4
RET_DK = 256
RET_DV = 512
RET_QK_W = RET_HEADS * RET_DK
RET_V_W = RET_HEADS * RET_DV
MLP_W = 4 * D_MODEL
ROPE_BASE = 10000.0
EPS = 1e-6

RET_CHUNK = 256
TOKEN_BLOCK = 512
CONV_BLOCK = 256
CONV_TILE = 64
CONV_HALO = 16
LANES = 128
MOD_ROWS = 16
VMEM_LIMIT_BYTES = 58 * 1024 * 1024

BF16 = jnp.bfloat16
F32 = jnp.float32


def _dot(a, b):
    return jnp.dot(a, b, preferred_element_type=F32)


def _resident(shape):
    zeros = (0,) * len(shape)
    return pl.BlockSpec(shape, lambda *_: zeros, pipeline_mode=pl.Buffered(1))


def _params(sem):
    return pltpu.CompilerParams(dimension_semantics=sem, vmem_limit_bytes=VMEM_LIMIT_BYTES)


def _mod_kernel(cond_ref, w_ref, b_ref, dec_ref, mod_ref, lg_ref):
    c = cond_ref[...]
    s = (c * jax.nn.sigmoid(c)).astype(BF16)
    mod_ref[...] = _dot(s, w_ref[...].astype(BF16)) + b_ref[...]
    lg_ref[...] = jax.nn.log_sigmoid(dec_ref[...])


def _modulation(cond, w_mod, b_mod, decay):
    depth = w_mod.shape[0]
    nj = w_mod.shape[2] // D_MODEL
    return pl.pallas_call(
        _mod_kernel,
        grid=(depth, nj),
        in_specs=[
            pl.BlockSpec((MOD_ROWS, D_MODEL), lambda l, j: (0, 0)),
            pl.BlockSpec((None, D_MODEL, D_MODEL), lambda l, j: (l, 0, j)),
            pl.BlockSpec((None, 1, D_MODEL), lambda l, j: (l, 0, j)),
            pl.BlockSpec(decay.shape, lambda l, j: (0, 0)),
        ],
        out_specs=[
            pl.BlockSpec((None, MOD_ROWS, D_MODEL), lambda l, j: (l, 0, j)),
            pl.BlockSpec(decay.shape, lambda l, j: (0, 0)),
        ],
        out_shape=[
            jax.ShapeDtypeStruct((depth, MOD_ROWS, nj * D_MODEL), F32),
            jax.ShapeDtypeStruct(decay.shape, F32),
        ],
        compiler_params=_params(("arbitrary", "arbitrary")),
        name="modulation",
    )(cond, w_mod, b_mod.reshape(depth, 1, -1), decay)


def _inproj_kernel(*refs, rope):
    if rope:
        (x_ref, mod_ref, g_ref, cos_ref, sin_ref, cost_ref, sint_ref,
         wu_ref, wq_ref, wkt_ref, wv_ref, wrg_ref, wbg_ref,
         a_ref, q_ref, kt_ref, v_ref, rgs_ref, gab_ref) = refs
    else:
        (x_ref, mod_ref, g_ref,
         wu_ref, wq_ref, wkt_ref, wv_ref, wrg_ref, wbg_ref,
         a_ref, q_ref, kt_ref, v_ref, rgs_ref, gab_ref) = refs
    x = x_ref[...]
    ms = jnp.mean(x * x, axis=-1, keepdims=True)
    y = x * lax.rsqrt(ms + EPS) * g_ref[...]
    h = (y * (1.0 + mod_ref[1:2, :]) + mod_ref[0:1, :]).astype(BF16)

    half = CONV_CH // 2
    for c in range(2):
        u1 = _dot(h, wu_ref[:, c * half:(c + 1) * half])
        u2 = _dot(h, wu_ref[:, CONV_CH + c * half:CONV_CH + (c + 1) * half])
        a_ref[:, c * half:(c + 1) * half] = (u1 * jax.nn.sigmoid(u2)).astype(BF16)

    hd = RET_DK // 2
    for hh in range(RET_HEADS):
        qh = _dot(h, wq_ref[:, hh * RET_DK:(hh + 1) * RET_DK])
        if rope:
            x1, x2 = qh[:, :hd], qh[:, hd:]
            cos, sin = cos_ref[...], sin_ref[...]
            q_ref[:, hh * RET_DK:hh * RET_DK + hd] = (x1 * cos - x2 * sin).astype(BF16)
            q_ref[:, hh * RET_DK + hd:(hh + 1) * RET_DK] = (x2 * cos + x1 * sin).astype(BF16)
        else:
            q_ref[:, hh * RET_DK:(hh + 1) * RET_DK] = qh.astype(BF16)

    nck = kt_ref.shape[0]
    for hh in range(RET_HEADS):
        kh = lax.dot_general(wkt_ref[hh * RET_DK:(hh + 1) * RET_DK, :], h,
                             (((1,), (1,)), ((), ())), preferred_element_type=F32)
        kh = kh * (RET_DK ** -0.5)
        if rope:
            x1, x2 = kh[:hd, :], kh[hd:, :]
            cos, sin = cost_ref[...], sint_ref[...]
            kh = jnp.concatenate([x1 * cos - x2 * sin, x2 * cos + x1 * sin], axis=0)
        kh = kh.astype(BF16)
        for j in range(nck):
            kt_ref[j, hh * RET_DK:(hh + 1) * RET_DK, :] = kh[:, j * RET_CHUNK:(j + 1) * RET_CHUNK]

    cw = 512
    for c in range(RET_V_W // cw):
        sl = slice(c * cw, (c + 1) * cw)
        v_ref[:, sl] = _dot(h, wv_ref[:, sl]).astype(BF16)
    for c in range(RET_V_W // cw):
        sl = slice(c * cw, (c + 1) * cw)
        r = _dot(h, wrg_ref[:, sl])
        rgs_ref[:, sl] = (r * jax.nn.sigmoid(r)).astype(BF16)
    for c in range(2 * D_MODEL // cw):
        sl = slice(c * cw, (c + 1) * cw)
        gab_ref[:, sl] = jax.nn.sigmoid(_dot(h, wbg_ref[:, sl])).astype(BF16)


def _inproj(x, mod, layer, g, w, tables, seq_len):
    n = x.shape[0]
    bt = TOKEN_BLOCK
    rope = tables is not None
    per_seq = seq_len // bt
    tok = lambda width: pl.BlockSpec((bt, width), lambda i: (i, 0))
    if rope:
        mod_spec = pl.BlockSpec((None, None, 6, D_MODEL), lambda i: (layer, 1 + i // per_seq, 0, 0))
    else:
        mod_spec = pl.BlockSpec((None, None, 6, D_MODEL), lambda i: (layer, 0, 0, 0))
    in_specs = [tok(D_MODEL), mod_spec, _resident((1, D_MODEL))]
    args = [x, mod, g]
    if rope:
        cos, sin, cost, sint = tables
        in_specs += [
            pl.BlockSpec((bt, RET_DK // 2), lambda i: (i % per_seq, 0)),
            pl.BlockSpec((bt, RET_DK // 2), lambda i: (i % per_seq, 0)),
            pl.BlockSpec((RET_DK // 2, bt), lambda i: (0, i % per_seq)),
            pl.BlockSpec((RET_DK // 2, bt), lambda i: (0, i % per_seq)),
        ]
        args += [cos, sin, cost, sint]
    in_specs += [_resident(wi.shape) for wi in w]
    args += list(w)
    nck = bt // RET_CHUNK
    out_specs = [
        tok(CONV_CH), tok(RET_QK_W),
        pl.BlockSpec((nck, RET_QK_W, RET_CHUNK), lambda i: (i, 0, 0)),
        tok(RET_V_W), tok(RET_V_W), tok(2 * D_MODEL),
    ]
    out_shape = [
        jax.ShapeDtypeStruct((n, CONV_CH), BF16),
        jax.ShapeDtypeStruct((n, RET_QK_W), BF16),
        jax.ShapeDtypeStruct((n // RET_CHUNK, RET_QK_W, RET_CHUNK), BF16),
        jax.ShapeDtypeStruct((n, RET_V_W), BF16),
        jax.ShapeDtypeStruct((n, RET_V_W), BF16),
        jax.ShapeDtypeStruct((n, 2 * D_MODEL), BF16),
    ]
    return pl.pallas_call(
        functools.partial(_inproj_kernel, rope=rope),
        grid=(n // bt,),
        in_specs=in_specs,
        out_specs=out_specs,
        out_shape=out_shape,
        compiler_params=_params(("arbitrary",)),
        name="inproj_lat" if rope else "inproj_ctx",
    )(*args)


def _conv_kernel(a_ref, w_ref, b_ref, g_ref, o_ref, pad_scr, conv_scr, *, seg):
    bt = a_ref.shape[0]
    nseg = bt // seg
    pseg = seg + 2 * CONV_HALO
    nlc = CONV_CH // LANES

    @pl.when(pl.program_id(0) == 0)
    def _():
        zeros = jnp.zeros((nlc, CONV_HALO, LANES), F32)
        for s in range(nseg):
            pad_scr[:, s * pseg:s * pseg + CONV_HALO, :] = zeros
            pad_scr[:, s * pseg + CONV_HALO + seg:(s + 1) * pseg, :] = zeros

    for s in range(nseg):
        for lc in range(nlc):
            pad_scr[lc, s * pseg + CONV_HALO:s * pseg + CONV_HALO + seg, :] = (
                a_ref[s * seg:(s + 1) * seg, lc * LANES:(lc + 1) * LANES].astype(F32))

    tiles_per_seg = seg // CONV_TILE

    def chunk(lc, carry):
        for t in range(bt // CONV_TILE):
            s = t // tiles_per_seg
            o = (t % tiles_per_seg) * CONV_TILE
            base = s * pseg + o + (CONV_HALO - CONV_PAD)
            acc = jnp.zeros((CONV_TILE, LANES), F32)
            for tap in range(CONV_W):
                acc = acc + pad_scr[lc, base + tap:base + tap + CONV_TILE, :] * w_ref[lc, tap:tap + 1, :]
            conv_scr[lc, t * CONV_TILE:(t + 1) * CONV_TILE, :] = acc
        return carry

    lax.fori_loop(0, nlc, chunk, 0)
    c = conv_scr[...] + b_ref[...]
    ssq = jnp.sum(jnp.sum(c * c, axis=0), axis=-1, keepdims=True)
    scale = lax.rsqrt(ssq * (1.0 / CONV_CH) + EPS)
    for lc in range(nlc):
        cn = c[lc] * scale * g_ref[lc]
        o_ref[:, lc * LANES:(lc + 1) * LANES] = (cn * jax.nn.sigmoid(cn)).astype(BF16)


def _conv_branch(a, w, b, g, seg):
    n = a.shape[0]
    bt = CONV_BLOCK
    nseg = bt // seg
    nlc = CONV_CH // LANES
    w3 = w.reshape(CONV_W, nlc, LANES).transpose(1, 0, 2)
    b3 = b.reshape(nlc, 1, LANES)
    g3 = g.reshape(nlc, 1, LANES)
    return pl.pallas_call(
        functools.partial(_conv_kernel, seg=seg),
        grid=(n // bt,),
        in_specs=[pl.BlockSpec((bt, CONV_CH), lambda i: (i, 0)),
                  _resident(w3.shape), _resident(b3.shape), _resident(g3.shape)],
        out_specs=pl.BlockSpec((bt, CONV_CH), lambda i: (i, 0)),
        out_shape=jax.ShapeDtypeStruct((n, CONV_CH), BF16),
        scratch_shapes=[
            pltpu.VMEM((nlc, nseg * (seg + 2 * CONV_HALO), LANES), F32),
            pltpu.VMEM((nlc, bt, LANES), F32),
        ],
        compiler_params=_params(("arbitrary",)),
        name=f"conv_seg{seg}",
    )(a, w3, b3, g3)


def _decay_tables(lgf, lgb, qd_scr, kd_scr, dm_scr):
    c = RET_CHUNK
    ri = lax.broadcasted_iota(jnp.int32, (c, RET_DK), 0).astype(F32)
    qd_scr[0] = jnp.exp((ri + 1.0) * lgf)
    qd_scr[1] = jnp.exp((c - ri) * lgb)
    ci = lax.broadcasted_iota(jnp.int32, (RET_DK, c), 1).astype(F32)
    kd_scr[0] = jnp.exp((c - 1.0 - ci) * lgf)
    kd_scr[1] = jnp.exp(ci * lgb)
    di = (lax.broadcasted_iota(jnp.int32, (c, c), 0) - lax.broadcasted_iota(jnp.int32, (c, c), 1)).astype(F32)
    dm_scr[...] = (jnp.where(di >= 0, jnp.exp(jnp.maximum(di, 0.0) * lgf), 0.0)
                   + jnp.where(di <= 0, jnp.exp(jnp.maximum(-di, 0.0) * lgb), 0.0))


def _head_norm_gate(y, rgs):
    mu = jnp.mean(y, axis=-1, keepdims=True)
    yc = y - mu
    var = jnp.mean(yc * yc, axis=-1, keepdims=True)
    return (rgs.astype(F32) * (yc * lax.rsqrt(var + EPS))).astype(BF16)


def _ret_lat_kernel(lg_ref, q_ref, kt_ref, v_ref, rgs_ref, s0_ref, yg_ref,
                    qd_scr, kd_scr, dm_scr, sf_scr, sb_scr, sbh_scr, *, layer, nchunk):
    c = RET_CHUNK
    h = pl.program_id(1)
    lgf = lg_ref[layer * 2 * RET_HEADS + h]
    lgb = lg_ref[layer * 2 * RET_HEADS + RET_HEADS + h]
    _decay_tables(lgf, lgb, qd_scr, kd_scr, dm_scr)
    sdf = jnp.exp(jnp.full((1, RET_DV), c * lgf, F32))
    sdb = jnp.exp(jnp.full((1, RET_DV), c * lgb, F32))

    def rows(i):
        return pl.ds(pl.multiple_of(i * c, c), c)

    sb_scr[...] = s0_ref[1]

    def bwd(i, carry):
        ck = nchunk - 1 - i
        sbh_scr[ck] = sb_scr[...].astype(BF16)
        kd = (kt_ref[ck].astype(F32) * kd_scr[1]).astype(BF16)
        sb_scr[...] = sb_scr[...] * sdb + _dot(kd, v_ref[rows(ck), :])
        return carry

    lax.fori_loop(0, nchunk - 1, bwd, 0)
    sbh_scr[0] = sb_scr[...].astype(BF16)

    sf_scr[...] = s0_ref[0]

    def fwd(ck, carry):
        qc = q_ref[rows(ck), :]
        kc = kt_ref[ck]
        vc = v_ref[rows(ck), :]
        pm = (_dot(qc, kc) * dm_scr[...]).astype(BF16)
        qf = qc.astype(F32)
        y = (_dot(pm, vc)
             + _dot((qf * qd_scr[0]).astype(BF16), sf_scr[...].astype(BF16))
             + _dot((qf * qd_scr[1]).astype(BF16), sbh_scr[ck]))
        yg_ref[rows(ck), :] = _head_norm_gate(y, rgs_ref[rows(ck), :])

        @pl.when(ck < nchunk - 1)
        def _():
            kd = (kc.astype(F32) * kd_scr[0]).astype(BF16)
            sf_scr[...] = sf_scr[...] * sdf + _dot(kd, vc)
        return carry

    lax.fori_loop(0, nchunk, fwd, 0)


def _ret_ctx_kernel(lg_ref, q_ref, kt_ref, v_ref, rgs_ref, yg_ref, st_ref,
                    qd_scr, kd_scr, dm_scr, *, layer):
    h = pl.program_id(1)
    lgf = lg_ref[layer * 2 * RET_HEADS + h]
    lgb = lg_ref[layer * 2 * RET_HEADS + RET_HEADS + h]
    _decay_tables(lgf, lgb, qd_scr, kd_scr, dm_scr)
    qc = q_ref[...]
    kc = kt_ref[0]
    vc = v_ref[...]
    pm = (_dot(qc, kc) * dm_scr[...]).astype(BF16)
    yg_ref[...] = _head_norm_gate(_dot(pm, vc), rgs_ref[...])
    kf = kc.astype(F32)
    st_ref[0] = _dot((kf * kd_scr[0]).astype(BF16), vc)
    st_ref[1] = _dot((kf * kd_scr[1]).astype(BF16), vc)


def _ret_scratch():
    c = RET_CHUNK
    return [pltpu.VMEM((2, c, RET_DK), F32), pltpu.VMEM((2, RET_DK, c), F32), pltpu.VMEM((c, c), F32)]


def _ret_common_specs(seq_len):
    nchunk = seq_len // RET_CHUNK
    return [
        pl.BlockSpec(memory_space=pltpu.SMEM),
        pl.BlockSpec((seq_len, RET_DK), lambda b, h: (b, h)),
        pl.BlockSpec((nchunk, RET_DK, RET_CHUNK), lambda b, h: (b, h, 0)),
        pl.BlockSpec((seq_len, RET_DV), lambda b, h: (b, h)),
        pl.BlockSpec((seq_len, RET_DV), lambda b, h: (b, h)),
    ]


def _retention_lat(lg, q, kt, v, rgs, state_ret, layer, seq_len):
    n = q.shape[0]
    nchunk = seq_len // RET_CHUNK
    return pl.pallas_call(
        functools.partial(_ret_lat_kernel, layer=layer, nchunk=nchunk),
        grid=(n // seq_len, RET_HEADS),
        in_specs=_ret_common_specs(seq_len) + [
            pl.BlockSpec((None, None, 2, None, RET_DK, RET_DV), lambda b, h: (b, layer, 0, h, 0, 0)),
        ],
        out_specs=pl.BlockSpec((seq_len, RET_DV), lambda b, h: (b, h)),
        out_shape=jax.ShapeDtypeStruct((n, RET_V_W), BF16),
        scratch_shapes=_ret_scratch() + [
            pltpu.VMEM((RET_DK, RET_DV), F32),
            pltpu.VMEM((RET_DK, RET_DV), F32),
            pltpu.VMEM((nchunk, RET_DK, RET_DV), BF16),
        ],
        compiler_params=_params(("arbitrary", "arbitrary")),
        name="retention_lat",
    )(lg, q, kt, v, rgs, state_ret)


def _retention_ctx(lg, q, kt, v, rgs, layer, seq_len):
    assert seq_len == RET_CHUNK
    n = q.shape[0]
    nb = n // seq_len
    return pl.pallas_call(
        functools.partial(_ret_ctx_kernel, layer=layer),
        grid=(nb, RET_HEADS),
        in_specs=_ret_common_specs(seq_len),
        out_specs=[
            pl.BlockSpec((seq_len, RET_DV), lambda b, h: (b, h)),
            pl.BlockSpec((None, 2, None, RET_DK, RET_DV), lambda b, h: (b, 0, h, 0, 0)),
        ],
        out_shape=[
            jax.ShapeDtypeStruct((n, RET_V_W), BF16),
            jax.ShapeDtypeStruct((nb, 2, RET_HEADS, RET_DK, RET_DV), F32),
        ],
        scratch_shapes=_ret_scratch(),
        compiler_params=_params(("arbitrary", "arbitrary")),
        name="retention_ctx",
    )(lg, q, kt, v, rgs)


def _outmlp_kernel(x_ref, an_ref, yg_ref, gab_ref, mod_ref, g2_ref, gf_ref,
                   wco_ref, wro_ref, wo_ref, w1_ref, w2_ref, o_ref, *, final_norm):
    a = _dot(an_ref[...], wco_ref[...])
    b = _dot(yg_ref[...], wro_ref[...])
    m = (gab_ref[:, :D_MODEL].astype(F32) * a + gab_ref[:, D_MODEL:].astype(F32) * b).astype(BF16)
    x = x_ref[...] + mod_ref[2:3, :] * _dot(m, wo_ref[...])
    ms = jnp.mean(x * x, axis=-1, keepdims=True)
    h2 = (x * lax.rsqrt(ms + EPS) * g2_ref[...] * (1.0 + mod_ref[4:5, :]) + mod_ref[3:4, :]).astype(BF16)
    cw = 1024
    acc = None
    for c in range(MLP_W // cw):
        t = jnp.maximum(_dot(h2, w1_ref[:, c * cw:(c + 1) * cw]), 0.0)
        part = _dot((t * t).astype(BF16), w2_ref[c * cw:(c + 1) * cw, :])
        acc = part if acc is None else acc + part
    x = x + mod_ref[5:6, :] * acc
    if final_norm:
        ms = jnp.mean(x * x, axis=-1, keepdims=True)
        x = x * lax.rsqrt(ms + EPS) * gf_ref[...]
    o_ref[...] = x


def _outmlp(x, an, yg, gab, mod, layer, g2, gf, w, seq_len, latent, final_norm):
    n = x.shape[0]
    bt = TOKEN_BLOCK
    per_seq = seq_len // bt
    tok = lambda width: pl.BlockSpec((bt, width), lambda i: (i, 0))
    if latent:
        mod_spec = pl.BlockSpec((None, None, 6, D_MODEL), lambda i: (layer, 1 + i // per_seq, 0, 0))
    else:
        mod_spec = pl.BlockSpec((None, None, 6, D_MODEL), lambda i: (layer, 0, 0, 0))
    return pl.pallas_call(
        functools.partial(_outmlp_kernel, final_norm=final_norm),
        grid=(n // bt,),
        in_specs=[tok(D_MODEL), tok(CONV_CH), tok(RET_V_W), tok(2 * D_MODEL), mod_spec,
                  _resident((1, D_MODEL)), _resident((1, D_MODEL))] + [_resident(wi.shape) for wi in w],
        out_specs=tok(D_MODEL),
        out_shape=jax.ShapeDtypeStruct((n, D_MODEL), F32),
        compiler_params=_params(("arbitrary",)),
        name="outmlp_lat" if latent else "outmlp_ctx",
    )(x, an, yg, gab, mod, g2, gf, *w)


def _rope_tables(seq_len):
    t = jnp.arange(seq_len)
    row = (t // GRID_W).astype(F32)
    col = (t % GRID_W).astype(F32)
    nf = RET_DK // 4
    inv = ROPE_BASE ** (-jnp.arange(nf, dtype=F32) / nf)
    ang = jnp.concatenate([row[:, None] * inv, col[:, None] * inv], axis=-1)
    cos, sin = jnp.cos(ang), jnp.sin(ang)
    return cos, sin, cos.T, sin.T


def kernel(x_prompt, x_sample, c, state_ret, c_ctx, norm1_g, norm2_g, w_mod, b_mod, w_in, conv_dw, conv_b,
           conv_norm_g, w_conv_out, ret_decay_logit, w_ret_out, w_out, w_mlp1, w_mlp2, final_norm_g):
    depth = w_in.shape[0]
    nb_ctx, seq_ctx, _ = x_prompt.shape
    nb_lat, seq_lat, _ = x_sample.shape
    assert 1 + nb_lat <= MOD_ROWS
    assert TOKEN_BLOCK % seq_ctx == 0 and (nb_ctx * seq_ctx) % TOKEN_BLOCK == 0
    assert seq_lat % TOKEN_BLOCK == 0 and TOKEN_BLOCK % GRID_W == 0

    cond = jnp.concatenate([c_ctx[None, :], c, jnp.zeros((MOD_ROWS - 1 - nb_lat, D_MODEL), F32)], axis=0)
    mod, lg = _modulation(cond, w_mod, b_mod, ret_decay_logit.reshape(1, -1))
    mod = mod.reshape(depth, MOD_ROWS, 6, D_MODEL)
    lg = lg.reshape(-1)
    tables = _rope_tables(seq_lat)

    xp = x_prompt.reshape(nb_ctx * seq_ctx, D_MODEL)
    xs = x_sample.reshape(nb_lat * seq_lat, D_MODEL)
    row = lambda t: t.reshape(1, -1)
    o0, o1, o2, o3, o4 = 2 * CONV_CH, 2 * CONV_CH + RET_QK_W, 2 * CONV_CH + 2 * RET_QK_W, \
        2 * CONV_CH + 2 * RET_QK_W + RET_V_W, 2 * CONV_CH + 2 * RET_QK_W + 2 * RET_V_W
    states = []
    for l in range(depth):
        wl = w_in[l].astype(BF16)
        w_proj = (wl[:, :o0], wl[:, o0:o1], wl[:, o1:o2].T, wl[:, o2:o3], wl[:, o3:o4], wl[:, o4:])
        w_tail = tuple(t[l].astype(BF16) for t in (w_conv_out, w_ret_out, w_out, w_mlp1, w_mlp2))
        last = l == depth - 1
        g1, g2, gf = row(norm1_g[l]), row(norm2_g[l]), row(final_norm_g)
        cb, cg = conv_b[l], conv_norm_g[l]

        a, q, kt, v, rgs, gab = _inproj(xp, mod, l, g1, w_proj, None, seq_ctx)
        an = _conv_branch(a, conv_dw[l], cb, cg, seq_ctx)
        yg, st = _retention_ctx(lg, q, kt, v, rgs, l, seq_ctx)
        states.append(st)
        xp = _outmlp(xp, an, yg, gab, mod, l, g2, gf, w_tail, seq_ctx, False, last)

        a, q, kt, v, rgs, gab = _inproj(xs, mod, l, g1, w_proj, tables, seq_lat)
        an = _conv_branch(a, conv_dw[l], cb, cg, GRID_W)
        yg = _retention_lat(lg, q, kt, v, rgs, state_ret, l, seq_lat)
        xs = _outmlp(xs, an, yg, gab, mod, l, g2, gf, w_tail, seq_lat, True, last)

    new_state = jnp.stack(states, axis=1)
    return (xp.reshape(x_prompt.shape), xs.reshape(x_sample.shape), new_state)
```

```python
import functools
import math

import jax
import jax.numpy as jnp
import numpy as np
from jax import lax
from jax.experimental import pallas as pl
from jax.experimental.pallas import tpu as pltpu

D_MODEL = 1024
GRID_W = 64
CONV_CH = D_MODEL
CONV_W = 31
CONV_PAD = CONV_W // 2
RET_HEADS = 4
RET_DK = 256
RET_DV = 512
RET_QK_W = RET_HEADS * RET_DK
RET_V_W = RET_HEADS * RET_DV
MLP_W = 4 * D_MODEL
ROPE_BASE = 10000.0
EPS = 1e-6

RET_CHUNK = 256
TOKEN_BLOCK = 512
CONV_BLOCK = 256
CONV_TILE = 64
CONV_HALO = 16
LANES = 128
MOD_ROWS = 16
VMEM_LIMIT_BYTES = 58 * 1024 * 1024

BF16 = jnp.bfloat16
F32 = jnp.float32


def _dot(a, b):
    return jnp.dot(a, b, preferred_element_type=F32)


def _resident(shape):
    zeros = (0,) * len(shape)
    return pl.BlockSpec(shape, lambda *_: zeros, pipeline_mode=pl.Buffered(1))


def _params(sem):
    return pltpu.CompilerParams(dimension_semantics=sem, vmem_limit_bytes=VMEM_LIMIT_BYTES)


def _mod_kernel(cond_ref, w_ref, b_ref, dec_ref, mod_ref, lg_ref):
    c = cond_ref[...]
    s = (c * jax.nn.sigmoid(c)).astype(BF16)
    mod_ref[...] = _dot(s, w_ref[...].astype(BF16)) + b_ref[...]
    lg_ref[...] = jax.nn.log_sigmoid(dec_ref[...])


def _modulation(cond, w_mod, b_mod, decay):
    depth = w_mod.shape[0]
    nj = w_mod.shape[2] // D_MODEL
    return pl.pallas_call(
        _mod_kernel,
        grid=(depth, nj),
        in_specs=[
            pl.BlockSpec((MOD_ROWS, D_MODEL), lambda l, j: (0, 0)),
            pl.BlockSpec((None, D_MODEL, D_MODEL), lambda l, j: (l, 0, j)),
            pl.BlockSpec((None, 1, D_MODEL), lambda l, j: (l, 0, j)),
            pl.BlockSpec(decay.shape, lambda l, j: (0, 0)),
        ],
        out_specs=[
            pl.BlockSpec((None, MOD_ROWS, D_MODEL), lambda l, j: (l, 0, j)),
            pl.BlockSpec(decay.shape, lambda l, j: (0, 0)),
        ],
        out_shape=[
            jax.ShapeDtypeStruct((depth, MOD_ROWS, nj * D_MODEL), F32),
            jax.ShapeDtypeStruct(decay.shape, F32),
        ],
        compiler_params=_params(("arbitrary", "arbitrary")),
        name="modulation",
    )(cond, w_mod, b_mod.reshape(depth, 1, -1), decay)


def _inproj_kernel(*refs, rope):
    if rope:
        (x_ref, mod_ref, g_ref, cos_ref, sin_ref, cost_ref, sint_ref,
         wu_ref, wq_ref, wk_ref, wv_ref, wrg_ref, wbg_ref,
         a_ref, q_ref, kt_ref, v_ref, rgs_ref, gab_ref, wkt_ref) = refs
    else:
        (x_ref, mod_ref, g_ref,
         wu_ref, wq_ref, wk_ref, wv_ref, wrg_ref, wbg_ref,
         a_ref, q_ref, kt_ref, v_ref, rgs_ref, gab_ref, wkt_ref) = refs

    @pl.when(pl.program_id(0) == 0)
    def _():
        wkt_ref[...] = wk_ref[...].T

    x = x_ref[...]
    ms = jnp.mean(x * x, axis=-1, keepdims=True)
    y = x * lax.rsqrt(ms + EPS) * g_ref[...]
    h = (y * (1.0 + mod_ref[1:2, :]) + mod_ref[0:1, :]).astype(BF16)

    half = CONV_CH // 2
    for c in range(2):
        u1 = _dot(h, wu_ref[:, c * half:(c + 1) * half])
        u2 = _dot(h, wu_ref[:, CONV_CH + c * half:CONV_CH + (c + 1) * half])
        a_ref[:, c * half:(c + 1) * half] = (u1 * jax.nn.sigmoid(u2)).astype(BF16)

    hd = RET_DK // 2
    for hh in range(RET_HEADS):
        qh = _dot(h, wq_ref[:, hh * RET_DK:(hh + 1) * RET_DK])
        if rope:
            x1, x2 = qh[:, :hd], qh[:, hd:]
            cos, sin = cos_ref[...], sin_ref[...]
            q_ref[:, hh * RET_DK:hh * RET_DK + hd] = (x1 * cos - x2 * sin).astype(BF16)
            q_ref[:, hh * RET_DK + hd:(hh + 1) * RET_DK] = (x2 * cos + x1 * sin).astype(BF16)
        else:
            q_ref[:, hh * RET_DK:(hh + 1) * RET_DK] = qh.astype(BF16)

    nck = kt_ref.shape[0]
    for hh in range(RET_HEADS):
        kh = lax.dot_general(wkt_ref[hh * RET_DK:(hh + 1) * RET_DK, :], h,
                             (((1,), (1,)), ((), ())), preferred_element_type=F32)
        kh = kh * (RET_DK ** -0.5)
        if rope:
            x1, x2 = kh[:hd, :], kh[hd:, :]
            cos, sin = cost_ref[...], sint_ref[...]
            kh = jnp.concatenate([x1 * cos - x2 * sin, x2 * cos + x1 * sin], axis=0)
        kh = kh.astype(BF16)
        for j in range(nck):
            kt_ref[j, hh * RET_DK:(hh + 1) * RET_DK, :] = kh[:, j * RET_CHUNK:(j + 1) * RET_CHUNK]

    cw = 512
    for c in range(RET_V_W // cw):
        sl = slice(c * cw, (c + 1) * cw)
        v_ref[:, sl] = _dot(h, wv_ref[:, sl]).astype(BF16)
    for c in range(RET_V_W // cw):
        sl = slice(c * cw, (c + 1) * cw)
        r = _dot(h, wrg_ref[:, sl])
        rgs_ref[:, sl] = (r * jax.nn.sigmoid(r)).astype(BF16)
    for c in range(2 * D_MODEL // cw):
        sl = slice(c * cw, (c + 1) * cw)
        gab_ref[:, sl] = jax.nn.sigmoid(_dot(h, wbg_ref[:, sl])).astype(BF16)


def _layer_resident(shape, layer, *block):
    idx = (layer,) + (block if block else (0,) * len(shape))
    return pl.BlockSpec((None,) + tuple(shape), lambda *_: idx, pipeline_mode=pl.Buffered(1))


def _inproj(x, mod, layer, g, w_in, tables, seq_len):
    n = x.shape[0]
    bt = TOKEN_BLOCK
    rope = tables is not None
    per_seq = seq_len // bt
    tok = lambda width: pl.BlockSpec((bt, width), lambda i: (i, 0))
    if rope:
        mod_spec = pl.BlockSpec((None, None, 6, D_MODEL), lambda i: (layer, 1 + i // per_seq, 0, 0))
    else:
        mod_spec = pl.BlockSpec((None, None, 6, D_MODEL), lambda i: (layer, 0, 0, 0))
    in_specs = [tok(D_MODEL), mod_spec, _layer_resident((1, D_MODEL), layer)]
    args = [x, mod, g]
    if rope:
        cos, sin, cost, sint = tables
        in_specs += [
            pl.BlockSpec((bt, RET_DK // 2), lambda i: (i % per_seq, 0)),
            pl.BlockSpec((bt, RET_DK // 2), lambda i: (i % per_seq, 0)),
            pl.BlockSpec((RET_DK // 2, bt), lambda i: (0, i % per_seq)),
            pl.BlockSpec((RET_DK // 2, bt), lambda i: (0, i % per_seq)),
        ]
        args += [cos, sin, cost, sint]
    wide, narrow = (D_MODEL, 2 * D_MODEL), (D_MODEL, D_MODEL)
    assert 2 * CONV_CH == RET_V_W == 2 * D_MODEL and RET_QK_W == D_MODEL
    in_specs += [
        _layer_resident(wide, layer, 0, 0), _layer_resident(narrow, layer, 0, 2),
        _layer_resident(narrow, layer, 0, 3), _layer_resident(wide, layer, 0, 2),
        _layer_resident(wide, layer, 0, 3), _layer_resident(wide, layer, 0, 4),
    ]
    args += [w_in] * 6
    nck = bt // RET_CHUNK
    out_specs = [
        tok(CONV_CH), tok(RET_QK_W),
        pl.BlockSpec((nck, RET_QK_W, RET_CHUNK), lambda i: (i, 0, 0)),
        tok(RET_V_W), tok(RET_V_W), tok(2 * D_MODEL),
    ]
    out_shape = [
        jax.ShapeDtypeStruct((n, CONV_CH), BF16),
        jax.ShapeDtypeStruct((n, RET_QK_W), BF16),
        jax.ShapeDtypeStruct((n // RET_CHUNK, RET_QK_W, RET_CHUNK), BF16),
        jax.ShapeDtypeStruct((n, RET_V_W), BF16),
        jax.ShapeDtypeStruct((n, RET_V_W), BF16),
        jax.ShapeDtypeStruct((n, 2 * D_MODEL), BF16),
    ]
    return pl.pallas_call(
        functools.partial(_inproj_kernel, rope=rope),
        grid=(n // bt,),
        in_specs=in_specs,
        out_specs=out_specs,
        out_shape=out_shape,
        scratch_shapes=[pltpu.VMEM((RET_QK_W, D_MODEL), BF16)],
        compiler_params=_params(("arbitrary",)),
        name="inproj_lat" if rope else "inproj_ctx",
    )(*args)


def _conv_kernel(a_ref, w_ref, b_ref, g_ref, o_ref, pad_scr, conv_scr, *, seg):
    bt = a_ref.shape[0]
    nseg = bt // seg
    pseg = seg + 2 * CONV_HALO
    nlc = CONV_CH // LANES

    @pl.when(pl.program_id(0) == 0)
    def _():
        zeros = jnp.zeros((nlc, CONV_HALO, LANES), F32)
        for s in range(nseg):
            pad_scr[:, s * pseg:s * pseg + CONV_HALO, :] = zeros
            pad_scr[:, s * pseg + CONV_HALO + seg:(s + 1) * pseg, :] = zeros

    for s in range(nseg):
        for lc in range(nlc):
            pad_scr[lc, s * pseg + CONV_HALO:s * pseg + CONV_HALO + seg, :] = (
                a_ref[s * seg:(s + 1) * seg, lc * LANES:(lc + 1) * LANES].astype(F32))

    tiles_per_seg = seg // CONV_TILE

    def chunk(lc, carry):
        for t in range(bt // CONV_TILE):
            s = t // tiles_per_seg
            o = (t % tiles_per_seg) * CONV_TILE
            base = s * pseg + o + (CONV_HALO - CONV_PAD)
            acc = jnp.zeros((CONV_TILE, LANES), F32)
            for tap in range(CONV_W):
                acc = acc + pad_scr[lc, base + tap:base + tap + CONV_TILE, :] * w_ref[lc, tap:tap + 1, :]
            conv_scr[lc, t * CONV_TILE:(t + 1) * CONV_TILE, :] = acc
        return carry

    lax.fori_loop(0, nlc, chunk, 0)
    c = conv_scr[...] + b_ref[...]
    ssq = jnp.sum(jnp.sum(c * c, axis=0), axis=-1, keepdims=True)
    scale = lax.rsqrt(ssq * (1.0 / CONV_CH) + EPS)
    for lc in range(nlc):
        cn = c[lc] * scale * g_ref[lc]
        o_ref[:, lc * LANES:(lc + 1) * LANES] = (cn * jax.nn.sigmoid(cn)).astype(BF16)


def _conv_branch(a, w3, b3, g3, layer, seg):
    n = a.shape[0]
    bt = CONV_BLOCK
    nseg = bt // seg
    nlc = CONV_CH // LANES
    return pl.pallas_call(
        functools.partial(_conv_kernel, seg=seg),
        grid=(n // bt,),
        in_specs=[pl.BlockSpec((bt, CONV_CH), lambda i: (i, 0)),
                  _layer_resident(w3.shape[1:], layer), _layer_resident(b3.shape[1:], layer),
                  _layer_resident(g3.shape[1:], layer)],
        out_specs=pl.BlockSpec((bt, CONV_CH), lambda i: (i, 0)),
        out_shape=jax.ShapeDtypeStruct((n, CONV_CH), BF16),
        scratch_shapes=[
            pltpu.VMEM((nlc, nseg * (seg + 2 * CONV_HALO), LANES), F32),
            pltpu.VMEM((nlc, bt, LANES), F32),
        ],
        compiler_params=_params(("arbitrary",)),
        name=f"conv_seg{seg}",
    )(a, w3, b3, g3)


def _decay_tables(lgf, lgb, qd_scr, kd_scr, dm_scr):
    c = RET_CHUNK
    ri = lax.broadcasted_iota(jnp.int32, (c, RET_DK), 0).astype(F32)
    qd_scr[0] = jnp.exp((ri + 1.0) * lgf).astype(BF16)
    qd_scr[1] = jnp.exp((c - ri) * lgb).astype(BF16)
    ci = lax.broadcasted_iota(jnp.int32, (RET_DK, c), 1).astype(F32)
    kd_scr[0] = jnp.exp((c - 1.0 - ci) * lgf).astype(BF16)
    kd_scr[1] = jnp.exp(ci * lgb).astype(BF16)
    di = (lax.broadcasted_iota(jnp.int32, (c, c), 0) - lax.broadcasted_iota(jnp.int32, (c, c), 1)).astype(F32)
    dm_scr[...] = (jnp.where(di >= 0, jnp.exp(jnp.maximum(di, 0.0) * lgf), 0.0)
                   + jnp.where(di <= 0, jnp.exp(jnp.maximum(-di, 0.0) * lgb), 0.0))


def _head_norm_gate(y, rgs):
    mu = jnp.mean(y, axis=-1, keepdims=True)
    yc = y - mu
    var = jnp.mean(yc * yc, axis=-1, keepdims=True)
    return (rgs.astype(F32) * (yc * lax.rsqrt(var + EPS))).astype(BF16)


def _ret_lat_kernel(lg_ref, q_ref, kt_ref, v_ref, rgs_ref, s0_ref, yg_ref,
                    qd_scr, kd_scr, dm_scr, sf_scr, sb_scr, sfh_scr, sbh_scr, *, layer, nchunk):
    c = RET_CHUNK
    h = pl.program_id(1)
    lgf = lg_ref[layer * 2 * RET_HEADS + h]
    lgb = lg_ref[layer * 2 * RET_HEADS + RET_HEADS + h]
    _decay_tables(lgf, lgb, qd_scr, kd_scr, dm_scr)
    sdf = jnp.exp(jnp.full((1, RET_DV), c * lgf, F32))
    sdb = jnp.exp(jnp.full((1, RET_DV), c * lgb, F32))

    def rows(i):
        return slice(i * c, (i + 1) * c)

    sb_scr[...] = s0_ref[1]
    sbh_scr[nchunk - 1] = s0_ref[1].astype(BF16)
    for ck in range(nchunk - 1, 0, -1):
        new = sb_scr[...] * sdb + _dot(kt_ref[ck] * kd_scr[1], v_ref[rows(ck), :])
        sbh_scr[ck - 1] = new.astype(BF16)
        if ck > 1:
            sb_scr[...] = new

    sf_scr[...] = s0_ref[0]
    sfh_scr[0] = s0_ref[0].astype(BF16)
    for ck in range(nchunk):
        qc = q_ref[rows(ck), :]
        kc = kt_ref[ck]
        vc = v_ref[rows(ck), :]
        pm = (_dot(qc, kc) * dm_scr[...]).astype(BF16)
        y = _dot(pm, vc) + _dot(qc * qd_scr[0], sfh_scr[ck]) + _dot(qc * qd_scr[1], sbh_scr[ck])
        yg_ref[rows(ck), :] = _head_norm_gate(y, rgs_ref[rows(ck), :])
        if ck < nchunk - 1:
            new = sf_scr[...] * sdf + _dot(kc * kd_scr[0], vc)
            sfh_scr[ck + 1] = new.astype(BF16)
            if ck < nchunk - 2:
                sf_scr[...] = new


def _ret_ctx_kernel(*refs, layer, aliased):
    if aliased:
        refs = refs[:5] + refs[6:]
    lg_ref, q_ref, kt_ref, v_ref, rgs_ref, yg_ref, st_ref, qd_scr, kd_scr, dm_scr = refs
    h = pl.program_id(1)
    lgf = lg_ref[layer * 2 * RET_HEADS + h]
    lgb = lg_ref[layer * 2 * RET_HEADS + RET_HEADS + h]
    _decay_tables(lgf, lgb, qd_scr, kd_scr, dm_scr)
    qc = q_ref[...]
    kc = kt_ref[0]
    vc = v_ref[...]
    pm = (_dot(qc, kc) * dm_scr[...]).astype(BF16)
    yg_ref[...] = _head_norm_gate(_dot(pm, vc), rgs_ref[...])
    st_ref[0] = _dot(kc * kd_scr[0], vc)
    st_ref[1] = _dot(kc * kd_scr[1], vc)


def _ret_scratch():
    c = RET_CHUNK
    return [pltpu.VMEM((2, c, RET_DK), BF16), pltpu.VMEM((2, RET_DK, c), BF16), pltpu.VMEM((c, c), F32)]


def _ret_common_specs(seq_len):
    nchunk = seq_len // RET_CHUNK
    return [
        pl.BlockSpec(memory_space=pltpu.SMEM),
        pl.BlockSpec((seq_len, RET_DK), lambda b, h: (b, h)),
        pl.BlockSpec((nchunk, RET_DK, RET_CHUNK), lambda b, h: (b, h, 0)),
        pl.BlockSpec((seq_len, RET_DV), lambda b, h: (b, h)),
        pl.BlockSpec((seq_len, RET_DV), lambda b, h: (b, h)),
    ]


def _retention_lat(lg, q, kt, v, rgs, state_ret, layer, seq_len):
    n = q.shape[0]
    nchunk = seq_len // RET_CHUNK
    return pl.pallas_call(
        functools.partial(_ret_lat_kernel, layer=layer, nchunk=nchunk),
        grid=(n // seq_len, RET_HEADS),
        in_specs=_ret_common_specs(seq_len) + [
            pl.BlockSpec((None, None, 2, None, RET_DK, RET_DV), lambda b, h: (b, layer, 0, h, 0, 0)),
        ],
        out_specs=pl.BlockSpec((seq_len, RET_DV), lambda b, h: (b, h)),
        out_shape=jax.ShapeDtypeStruct((n, RET_V_W), BF16),
        scratch_shapes=_ret_scratch() + [
            pltpu.VMEM((RET_DK, RET_DV), F32),
            pltpu.VMEM((RET_DK, RET_DV), F32),
            pltpu.VMEM((nchunk, RET_DK, RET_DV), BF16),
            pltpu.VMEM((nchunk, RET_DK, RET_DV), BF16),
        ],
        compiler_params=_params(("arbitrary", "arbitrary")),
        name="retention_lat",
    )(lg, q, kt, v, rgs, state_ret)


def _retention_ctx(lg, q, kt, v, rgs, states, layer, depth, seq_len):
    assert seq_len == RET_CHUNK
    n = q.shape[0]
    nb = n // seq_len
    aliased = states is not None
    in_specs = _ret_common_specs(seq_len)
    args = [lg, q, kt, v, rgs]
    if aliased:
        in_specs.append(pl.BlockSpec(memory_space=pl.ANY))
        args.append(states)
    return pl.pallas_call(
        functools.partial(_ret_ctx_kernel, layer=layer, aliased=aliased),
        grid=(nb, RET_HEADS),
        in_specs=in_specs,
        out_specs=[
            pl.BlockSpec((seq_len, RET_DV), lambda b, h: (b, h)),
            pl.BlockSpec((None, None, 2, None, RET_DK, RET_DV), lambda b, h: (b, layer, 0, h, 0, 0)),
        ],
        out_shape=[
            jax.ShapeDtypeStruct((n, RET_V_W), BF16),
            jax.ShapeDtypeStruct((nb, depth, 2, RET_HEADS, RET_DK, RET_DV), F32),
        ],
        input_output_aliases={5: 1} if aliased else {},
        scratch_shapes=_ret_scratch(),
        compiler_params=_params(("arbitrary", "arbitrary")),
        name="retention_ctx",
    )(*args)


def _outmlp_kernel(x_ref, an_ref, yg_ref, gab_ref, mod_ref, g2_ref, gf_ref,
                   wco_ref, wro_ref, wo_ref, w1_ref, w2_ref, o_ref, *, final_norm):
    a = _dot(an_ref[...], wco_ref[...])
    b = _dot(yg_ref[...], wro_ref[...])
    m = (gab_ref[:, :D_MODEL].astype(F32) * a + gab_ref[:, D_MODEL:].astype(F32) * b).astype(BF16)
    x = x_ref[...] + mod_ref[2:3, :] * _dot(m, wo_ref[...])
    ms = jnp.mean(x * x, axis=-1, keepdims=True)
    h2 = (x * lax.rsqrt(ms + EPS) * g2_ref[...] * (1.0 + mod_ref[4:5, :]) + mod_ref[3:4, :]).astype(BF16)
    cw = 1024
    acc = None
    for c in range(MLP_W // cw):
        t = jnp.maximum(_dot(h2, w1_ref[:, c * cw:(c + 1) * cw]), 0.0)
        part = _dot((t * t).astype(BF16), w2_ref[c * cw:(c + 1) * cw, :])
        acc = part if acc is None else acc + part
    x = x + mod_ref[5:6, :] * acc
    if final_norm:
        ms = jnp.mean(x * x, axis=-1, keepdims=True)
        x = x * lax.rsqrt(ms + EPS) * gf_ref[...]
    o_ref[...] = x


def _outmlp(x, an, yg, gab, mod, layer, g2, gf, w, seq_len, latent, final_norm):
    n = x.shape[0]
    bt = TOKEN_BLOCK
    per_seq = seq_len // bt
    tok = lambda width: pl.BlockSpec((bt, width), lambda i: (i, 0))
    if latent:
        mod_spec = pl.BlockSpec((None, None, 6, D_MODEL), lambda i: (layer, 1 + i // per_seq, 0, 0))
    else:
        mod_spec = pl.BlockSpec((None, None, 6, D_MODEL), lambda i: (layer, 0, 0, 0))
    return pl.pallas_call(
        functools.partial(_outmlp_kernel, final_norm=final_norm),
        grid=(n // bt,),
        in_specs=[tok(D_MODEL), tok(CONV_CH), tok(RET_V_W), tok(2 * D_MODEL), mod_spec,
                  _layer_resident((1, D_MODEL), layer), _resident((1, D_MODEL))]
                 + [_layer_resident(wi.shape[1:], layer) for wi in w],
        out_specs=tok(D_MODEL),
        out_shape=jax.ShapeDtypeStruct((n, D_MODEL), F32),
        compiler_params=_params(("arbitrary",)),
        name="outmlp_lat" if latent else "outmlp_ctx",
    )(x, an, yg, gab, mod, g2, gf, *w)


def _rope_tables(seq_len):
    t = np.arange(seq_len)
    row = (t // GRID_W).astype(np.float32)
    col = (t % GRID_W).astype(np.float32)
    nf = RET_DK // 4
    inv = (np.float32(ROPE_BASE) ** (-np.arange(nf, dtype=np.float32) / np.float32(nf))).astype(np.float32)
    ang = np.concatenate([row[:, None] * inv, col[:, None] * inv], axis=-1).astype(np.float32)
    cos = np.cos(ang.astype(np.float64)).astype(np.float32)
    sin = np.sin(ang.astype(np.float64)).astype(np.float32)
    return tuple(jnp.asarray(t) for t in (cos, sin, np.ascontiguousarray(cos.T), np.ascontiguousarray(sin.T)))


def kernel(x_prompt, x_sample, c, state_ret, c_ctx, norm1_g, norm2_g, w_mod, b_mod, w_in, conv_dw, conv_b,
           conv_norm_g, w_conv_out, ret_decay_logit, w_ret_out, w_out, w_mlp1, w_mlp2, final_norm_g):
    depth = w_in.shape[0]
    nb_ctx, seq_ctx, _ = x_prompt.shape
    nb_lat, seq_lat, _ = x_sample.shape
    assert 1 + nb_lat <= MOD_ROWS
    assert TOKEN_BLOCK % seq_ctx == 0 and (nb_ctx * seq_ctx) % TOKEN_BLOCK == 0
    assert seq_lat % TOKEN_BLOCK == 0 and TOKEN_BLOCK % GRID_W == 0

    cond = jnp.concatenate([c_ctx[None, :], c, jnp.zeros((MOD_ROWS - 1 - nb_lat, D_MODEL), F32)], axis=0)
    mod, lg = _modulation(cond, w_mod, b_mod, ret_decay_logit.reshape(1, -1))
    mod = mod.reshape(depth, MOD_ROWS, 6, D_MODEL)
    lg = lg.reshape(-1)
    tables = _rope_tables(seq_lat)

    xp = x_prompt.reshape(nb_ctx * seq_ctx, D_MODEL)
    xs = x_sample.reshape(nb_lat * seq_lat, D_MODEL)
    w_in_h = w_in.astype(BF16)
    w_tail = tuple(t.astype(BF16) for t in (w_conv_out, w_ret_out, w_out, w_mlp1, w_mlp2))
    g1 = norm1_g.reshape(depth, 1, D_MODEL)
    g2 = norm2_g.reshape(depth, 1, D_MODEL)
    gf = final_norm_g.reshape(1, D_MODEL)
    nlc = CONV_CH // LANES
    cw3 = conv_dw.reshape(depth, CONV_W, nlc, LANES).transpose(0, 2, 1, 3)
    cb3 = conv_b.reshape(depth, nlc, 1, LANES)
    cg3 = conv_norm_g.reshape(depth, nlc, 1, LANES)

    new_state = None
    for l in range(depth):
        last = l == depth - 1
        a, q, kt, v, rgs, gab = _inproj(xp, mod, l, g1, w_in_h, None, seq_ctx)
        an = _conv_branch(a, cw3, cb3, cg3, l, seq_ctx)
        yg, new_state = _retention_ctx(lg, q, kt, v, rgs, new_state, l, depth, seq_ctx)
        xp = _outmlp(xp, an, yg, gab, mod, l, g2, gf, w_tail, seq_ctx, False, last)

        a, q, kt, v, rgs, gab = _inproj(xs, mod, l, g1, w_in_h, tables, seq_lat)
        an = _conv_branch(a, cw3, cb3, cg3, l, GRID_W)
        yg = _retention_lat(lg, q, kt, v, rgs, state_ret, l, seq_lat)
        xs = _outmlp(xs, an, yg, gab, mod, l, g2, gf, w_tail, seq_lat, True, last)

    return (xp.reshape(x_prompt.shape), xs.reshape(x_sample.shape), new_state)
```

```python
import functools

import jax
import jax.numpy as jnp
import numpy as np
from jax import lax
from jax.experimental import pallas as pl
from jax.experimental.pallas import tpu as pltpu

D_MODEL = 1024
GRID_W = 64
CONV_CH = D_MODEL
CONV_W = 31
CONV_PAD = CONV_W // 2
RET_HEADS = 4
RET_DK = 256
RET_DV = 512
RET_QK_W = RET_HEADS * RET_DK
RET_V_W = RET_HEADS * RET_DV
MLP_W = 4 * D_MODEL
ROPE_BASE = 10000.0
EPS = 1e-6

RET_CHUNK = 256
TOKEN_BLOCK = 512
CONV_TILE = 64
CONV_HALO = 16
LANES = 128
SUBLANES = 8
MOD_ROWS = 16
VMEM_LIMIT_BYTES = 58 * 1024 * 1024

BF16 = jnp.bfloat16
F32 = jnp.float32


def _dot(a, b):
    return jnp.dot(a, b, preferred_element_type=F32)


def _resident(shape):
    zeros = (0,) * len(shape)
    return pl.BlockSpec(shape, lambda *_: zeros, pipeline_mode=pl.Buffered(1))


def _layer_resident(shape, layer, *block):
    idx = (layer,) + (block if block else (0,) * len(shape))
    return pl.BlockSpec((None,) + tuple(shape), lambda *_: idx, pipeline_mode=pl.Buffered(1))


def _params(sem):
    return pltpu.CompilerParams(dimension_semantics=sem, vmem_limit_bytes=VMEM_LIMIT_BYTES)


def _mod_kernel(cond_ref, w_ref, b_ref, dec_ref, mod_ref, lg_ref):
    c = cond_ref[...]
    s = (c * jax.nn.sigmoid(c)).astype(BF16)
    mod_ref[...] = _dot(s, w_ref[...].astype(BF16)) + b_ref[...]
    lg_ref[...] = jax.nn.log_sigmoid(dec_ref[...])


def _modulation(cond, w_mod, b_mod, decay):
    depth = w_mod.shape[0]
    nj = w_mod.shape[2] // D_MODEL
    return pl.pallas_call(
        _mod_kernel,
        grid=(depth, nj),
        in_specs=[
            pl.BlockSpec((MOD_ROWS, D_MODEL), lambda l, j: (0, 0)),
            pl.BlockSpec((None, D_MODEL, D_MODEL), lambda l, j: (l, 0, j)),
            pl.BlockSpec((None, 1, D_MODEL), lambda l, j: (l, 0, j)),
            pl.BlockSpec(decay.shape, lambda l, j: (0, 0)),
        ],
        out_specs=[
            pl.BlockSpec((None, MOD_ROWS, D_MODEL), lambda l, j: (l, 0, j)),
            pl.BlockSpec(decay.shape, lambda l, j: (0, 0)),
        ],
        out_shape=[
            jax.ShapeDtypeStruct((depth, MOD_ROWS, nj * D_MODEL), F32),
            jax.ShapeDtypeStruct(decay.shape, F32),
        ],
        compiler_params=_params(("arbitrary", "arbitrary")),
        name="modulation",
    )(cond, w_mod, b_mod.reshape(depth, 1, -1), decay)


def _inproj_kernel(*refs, rope, seg):
    if rope:
        (x_ref, mod_ref, g_ref, cos_ref, sin_ref, cost_ref, sint_ref,
         wu_ref, wq_ref, wk_ref, wv_ref, wrg_ref, wbg_ref, cw_ref, cb_ref, cg_ref,
         an_ref, q_ref, kt_ref, v_ref, rgs_ref, gab_ref, wkt_ref, h_scr, pad_scr, conv_scr) = refs
    else:
        (x_ref, mod_ref, g_ref,
         wu_ref, wq_ref, wk_ref, wv_ref, wrg_ref, wbg_ref, cw_ref, cb_ref, cg_ref,
         an_ref, q_ref, kt_ref, v_ref, rgs_ref, gab_ref, wkt_ref, h_scr, pad_scr, conv_scr) = refs
    bt = x_ref.shape[0]
    nseg = bt // seg
    pseg = seg + 2 * CONV_HALO
    nlc = CONV_CH // LANES

    @pl.when(pl.program_id(0) == 0)
    def _():
        wkt_ref[...] = wk_ref[...].T
        zeros = jnp.zeros((nlc, CONV_HALO, LANES), F32)
        for s in range(nseg):
            pad_scr[:, s * pseg:s * pseg + CONV_HALO, :] = zeros
            pad_scr[:, s * pseg + CONV_HALO + seg:(s + 1) * pseg, :] = zeros

    x = x_ref[...]
    ms = jnp.mean(x * x, axis=-1, keepdims=True)
    y = x * lax.rsqrt(ms + EPS) * g_ref[...]
    h_scr[...] = (y * (1.0 + mod_ref[1:2, :]) + mod_ref[0:1, :]).astype(BF16)

    half = CONV_CH // 2
    for c in range(2):
        u1 = _dot(h_scr[...], wu_ref[:, c * half:(c + 1) * half])
        u2 = _dot(h_scr[...], wu_ref[:, CONV_CH + c * half:CONV_CH + (c + 1) * half])
        a = u1 * jax.nn.sigmoid(u2)
        for p in range(half // LANES):
            lc = c * (half // LANES) + p
            for s in range(nseg):
                pad_scr[lc, s * pseg + CONV_HALO:s * pseg + CONV_HALO + seg, :] = (
                    a[s * seg:(s + 1) * seg, p * LANES:(p + 1) * LANES])

    def conv_tile(t):
        tiles_per_seg = seg // CONV_TILE
        base = (t // tiles_per_seg) * pseg + (t % tiles_per_seg) * CONV_TILE + (CONV_HALO - CONV_PAD)
        ssq = None
        for lc in range(nlc):
            acc = jnp.zeros((CONV_TILE, LANES), F32)
            for tap in range(CONV_W):
                acc = acc + pad_scr[lc, base + tap:base + tap + CONV_TILE, :] * cw_ref[lc, tap:tap + 1, :]
            acc = acc + cb_ref[lc]
            conv_scr[lc] = acc
            ssq = acc * acc if ssq is None else ssq + acc * acc
        scale = lax.rsqrt(jnp.sum(ssq, axis=-1, keepdims=True) * (1.0 / CONV_CH) + EPS)
        for lc in range(nlc):
            cn = conv_scr[lc] * scale * cg_ref[lc]
            an_ref[t * CONV_TILE:(t + 1) * CONV_TILE, lc * LANES:(lc + 1) * LANES] = (
                cn * jax.nn.sigmoid(cn)).astype(BF16)

    hd = RET_DK // 2
    nck = kt_ref.shape[0]
    cw = D_MODEL

    def q_proj():
        for hh in range(RET_HEADS):
            qh = _dot(h_scr[...], wq_ref[:, hh * RET_DK:(hh + 1) * RET_DK])
            if rope:
                x1, x2 = qh[:, :hd], qh[:, hd:]
                cos, sin = cos_ref[...], sin_ref[...]
                q_ref[:, hh * RET_DK:hh * RET_DK + hd] = (x1 * cos - x2 * sin).astype(BF16)
                q_ref[:, hh * RET_DK + hd:(hh + 1) * RET_DK] = (x2 * cos + x1 * sin).astype(BF16)
            else:
                q_ref[:, hh * RET_DK:(hh + 1) * RET_DK] = qh.astype(BF16)

    def k_proj():
        for hh in range(RET_HEADS):
            kh = lax.dot_general(wkt_ref[hh * RET_DK:(hh + 1) * RET_DK, :], h_scr[...],
                                 (((1,), (1,)), ((), ())), preferred_element_type=F32)
            kh = kh * (RET_DK ** -0.5)
            if rope:
                x1, x2 = kh[:hd, :], kh[hd:, :]
                cos, sin = cost_ref[...], sint_ref[...]
                kh = jnp.concatenate([x1 * cos - x2 * sin, x2 * cos + x1 * sin], axis=0)
            kh = kh.astype(BF16)
            for j in range(nck):
                kt_ref[j, hh * RET_DK:(hh + 1) * RET_DK, :] = kh[:, j * RET_CHUNK:(j + 1) * RET_CHUNK]

    def v_proj(c):
        sl = slice(c * cw, (c + 1) * cw)
        v_ref[:, sl] = _dot(h_scr[...], wv_ref[:, sl]).astype(BF16)

    def rg_proj(c):
        sl = slice(c * cw, (c + 1) * cw)
        r = _dot(h_scr[...], wrg_ref[:, sl])
        rgs_ref[:, sl] = (r * jax.nn.sigmoid(r)).astype(BF16)

    def bg_proj(c):
        sl = slice(c * cw, (c + 1) * cw)
        gab_ref[:, sl] = jax.nn.sigmoid(_dot(h_scr[...], wbg_ref[:, sl])).astype(BF16)

    projections = [q_proj, k_proj, functools.partial(v_proj, 0), functools.partial(v_proj, 1),
                   functools.partial(rg_proj, 0), functools.partial(rg_proj, 1),
                   functools.partial(bg_proj, 0), functools.partial(bg_proj, 1)]
    nphase = len(projections)
    ntile = bt // CONV_TILE
    for i, proj in enumerate(projections):
        proj()
        for t in range(i * ntile // nphase, (i + 1) * ntile // nphase):
            conv_tile(t)


def _inproj(x, mod, layer, g, w_in, conv, tables, seq_len, seg):
    n = x.shape[0]
    bt = TOKEN_BLOCK
    rope = tables is not None
    per_seq = seq_len // bt
    nseg = bt // seg
    nlc = CONV_CH // LANES
    tok = lambda width: pl.BlockSpec((bt, width), lambda i: (i, 0))
    if rope:
        mod_spec = pl.BlockSpec((None, None, 6, D_MODEL), lambda i: (layer, 1 + i // per_seq, 0, 0))
    else:
        mod_spec = pl.BlockSpec((None, None, 6, D_MODEL), lambda i: (layer, 0, 0, 0))
    in_specs = [tok(D_MODEL), mod_spec, _layer_resident((1, D_MODEL), layer)]
    args = [x, mod, g]
    if rope:
        cos, sin, cost, sint = tables
        in_specs += [
            pl.BlockSpec((bt, RET_DK // 2), lambda i: (i % per_seq, 0)),
            pl.BlockSpec((bt, RET_DK // 2), lambda i: (i % per_seq, 0)),
            pl.BlockSpec((RET_DK // 2, bt), lambda i: (0, i % per_seq)),
            pl.BlockSpec((RET_DK // 2, bt), lambda i: (0, i % per_seq)),
        ]
        args += [cos, sin, cost, sint]
    wide, narrow = (D_MODEL, 2 * D_MODEL), (D_MODEL, D_MODEL)
    assert 2 * CONV_CH == RET_V_W == 2 * D_MODEL and RET_QK_W == D_MODEL
    in_specs += [
        _layer_resident(wide, layer, 0, 0), _layer_resident(narrow, layer, 0, 2),
        _layer_resident(narrow, layer, 0, 3), _layer_resident(wide, layer, 0, 2),
        _layer_resident(wide, layer, 0, 3), _layer_resident(wide, layer, 0, 4),
    ]
    args += [w_in] * 6
    in_specs += [_layer_resident(t.shape[1:], layer) for t in conv]
    args += list(conv)
    nck = bt // RET_CHUNK
    out_specs = [
        tok(CONV_CH), tok(RET_QK_W),
        pl.BlockSpec((nck, RET_QK_W, RET_CHUNK), lambda i: (i, 0, 0)),
        tok(RET_V_W), tok(RET_V_W), tok(2 * D_MODEL),
    ]
    out_shape = [
        jax.ShapeDtypeStruct((n, CONV_CH), BF16),
        jax.ShapeDtypeStruct((n, RET_QK_W), BF16),
        jax.ShapeDtypeStruct((n // RET_CHUNK, RET_QK_W, RET_CHUNK), BF16),
        jax.ShapeDtypeStruct((n, RET_V_W), BF16),
        jax.ShapeDtypeStruct((n, RET_V_W), BF16),
        jax.ShapeDtypeStruct((n, 2 * D_MODEL), BF16),
    ]
    return pl.pallas_call(
        functools.partial(_inproj_kernel, rope=rope, seg=seg),
        grid=(n // bt,),
        in_specs=in_specs,
        out_specs=out_specs,
        out_shape=out_shape,
        scratch_shapes=[
            pltpu.VMEM((RET_QK_W, D_MODEL), BF16),
            pltpu.VMEM((bt, D_MODEL), BF16),
            pltpu.VMEM((nlc, nseg * (seg + 2 * CONV_HALO), LANES), F32),
            pltpu.VMEM((nlc, CONV_TILE, LANES), F32),
        ],
        compiler_params=_params(("arbitrary",)),
        name="inproj_lat" if rope else "inproj_ctx",
    )(*args)


def _decay_tables(lgf, lgb, qd_scr, kd_scr, dm_scr):
    c = RET_CHUNK
    if qd_scr is not None:
        ri = lax.broadcasted_iota(jnp.int32, (c, RET_DK), 0).astype(F32)
        qd_scr[0] = jnp.exp((ri + 1.0) * lgf).astype(BF16)
        qd_scr[1] = jnp.exp((c - ri) * lgb).astype(BF16)
    ci = lax.broadcasted_iota(jnp.int32, (RET_DK, c), 1).astype(F32)
    kd_scr[0] = jnp.exp((c - 1.0 - ci) * lgf).astype(BF16)
    kd_scr[1] = jnp.exp(ci * lgb).astype(BF16)
    di = (lax.broadcasted_iota(jnp.int32, (c, c), 0) - lax.broadcasted_iota(jnp.int32, (c, c), 1)).astype(F32)
    dm_scr[...] = (jnp.where(di >= 0, jnp.exp(jnp.maximum(di, 0.0) * lgf), 0.0)
                   + jnp.where(di <= 0, jnp.exp(jnp.maximum(-di, 0.0) * lgb), 0.0))


def _head_norm_gate(y, rgs):
    mu = jnp.mean(y, axis=-1, keepdims=True)
    yc = y - mu
    var = jnp.mean(yc * yc, axis=-1, keepdims=True)
    return (rgs.astype(F32) * (yc * lax.rsqrt(var + EPS))).astype(BF16)


def _ret_lat_kernel(lg_ref, q_ref, kt_ref, v_ref, rgs_ref, s0_ref, yg_ref,
                    qd_scr, kd_scr, dm_scr, sd_scr, sf_scr, sb_scr, sfh_scr, sbh_scr, *, layer, nchunk):
    c = RET_CHUNK
    h = pl.program_id(0)

    @pl.when(pl.program_id(1) == 0)
    def _():
        lgf = lg_ref[layer * 2 * RET_HEADS + h]
        lgb = lg_ref[layer * 2 * RET_HEADS + RET_HEADS + h]
        _decay_tables(lgf, lgb, qd_scr, kd_scr, dm_scr)
        sd_scr[0] = jnp.exp(jnp.full((SUBLANES, RET_DV), c * lgf, F32))
        sd_scr[1] = jnp.exp(jnp.full((SUBLANES, RET_DV), c * lgb, F32))

    sdf = sd_scr[0, 0:1, :]
    sdb = sd_scr[1, 0:1, :]

    def rows(i):
        return slice(i * c, (i + 1) * c)

    sb_scr[...] = s0_ref[1]
    sbh_scr[nchunk - 1] = s0_ref[1].astype(BF16)
    for ck in range(nchunk - 1, 0, -1):
        new = sb_scr[...] * sdb + _dot(kt_ref[ck] * kd_scr[1], v_ref[rows(ck), :])
        sbh_scr[ck - 1] = new.astype(BF16)
        if ck > 1:
            sb_scr[...] = new

    sf_scr[...] = s0_ref[0]
    sfh_scr[0] = s0_ref[0].astype(BF16)
    for ck in range(nchunk):
        qc = q_ref[rows(ck), :]
        kc = kt_ref[ck]
        vc = v_ref[rows(ck), :]
        pm = (_dot(qc, kc) * dm_scr[...]).astype(BF16)
        y = _dot(pm, vc) + _dot(qc * qd_scr[0], sfh_scr[ck]) + _dot(qc * qd_scr[1], sbh_scr[ck])
        yg_ref[rows(ck), :] = _head_norm_gate(y, rgs_ref[rows(ck), :])
        if ck < nchunk - 1:
            new = sf_scr[...] * sdf + _dot(kc * kd_scr[0], vc)
            sfh_scr[ck + 1] = new.astype(BF16)
            if ck < nchunk - 2:
                sf_scr[...] = new


def _retention_lat(lg, q, kt, v, rgs, state_ret, layer, seq_len):
    n = q.shape[0]
    c = RET_CHUNK
    nchunk = seq_len // c
    return pl.pallas_call(
        functools.partial(_ret_lat_kernel, layer=layer, nchunk=nchunk),
        grid=(RET_HEADS, n // seq_len),
        in_specs=[
            pl.BlockSpec(memory_space=pltpu.SMEM),
            pl.BlockSpec((seq_len, RET_DK), lambda h, b: (b, h)),
            pl.BlockSpec((nchunk, RET_DK, c), lambda h, b: (b, h, 0)),
            pl.BlockSpec((seq_len, RET_DV), lambda h, b: (b, h)),
            pl.BlockSpec((seq_len, RET_DV), lambda h, b: (b, h)),
            pl.BlockSpec((None, None, 2, None, RET_DK, RET_DV), lambda h, b: (b, layer, 0, h, 0, 0)),
        ],
        out_specs=pl.BlockSpec((seq_len, RET_DV), lambda h, b: (b, h)),
        out_shape=jax.ShapeDtypeStruct((n, RET_V_W), BF16),
        scratch_shapes=[
            pltpu.VMEM((2, c, RET_DK), BF16),
            pltpu.VMEM((2, RET_DK, c), BF16),
            pltpu.VMEM((c, c), F32),
            pltpu.VMEM((2, SUBLANES, RET_DV), F32),
            pltpu.VMEM((RET_DK, RET_DV), F32),
            pltpu.VMEM((RET_DK, RET_DV), F32),
            pltpu.VMEM((nchunk, RET_DK, RET_DV), BF16),
            pltpu.VMEM((nchunk, RET_DK, RET_DV), BF16),
        ],
        compiler_params=_params(("arbitrary", "arbitrary")),
        name="retention_lat",
    )(lg, q, kt, v, rgs, state_ret)


def _ret_ctx_kernel(*refs, layer, aliased):
    if aliased:
        refs = refs[:5] + refs[6:]
    lg_ref, q_ref, kt_ref, v_ref, rgs_ref, yg_ref, st_ref, kd_scr, dm_scr = refs

    @pl.when(pl.program_id(0) == 0)
    def _():
        for h in range(RET_HEADS):
            lgf = lg_ref[layer * 2 * RET_HEADS + h]
            lgb = lg_ref[layer * 2 * RET_HEADS + RET_HEADS + h]
            _decay_tables(lgf, lgb, None, kd_scr.at[h], dm_scr.at[h])

    for h in range(RET_HEADS):
        qc = q_ref[:, h * RET_DK:(h + 1) * RET_DK]
        kc = kt_ref[0, h * RET_DK:(h + 1) * RET_DK, :]
        vc = v_ref[:, h * RET_DV:(h + 1) * RET_DV]
        pm = (_dot(qc, kc) * dm_scr[h]).astype(BF16)
        yg_ref[:, h * RET_DV:(h + 1) * RET_DV] = _head_norm_gate(
            _dot(pm, vc), rgs_ref[:, h * RET_DV:(h + 1) * RET_DV])
        st_ref[0, h] = _dot(kc * kd_scr[h, 0], vc)
        st_ref[1, h] = _dot(kc * kd_scr[h, 1], vc)


def _retention_ctx(lg, q, kt, v, rgs, states, layer, depth, seq_len):
    c = RET_CHUNK
    assert seq_len == c
    n = q.shape[0]
    nb = n // seq_len
    aliased = states is not None
    in_specs = [
        pl.BlockSpec(memory_space=pltpu.SMEM),
        pl.BlockSpec((seq_len, RET_QK_W), lambda b: (b, 0)),
        pl.BlockSpec((1, RET_QK_W, c), lambda b: (b, 0, 0)),
        pl.BlockSpec((seq_len, RET_V_W), lambda b: (b, 0)),
        pl.BlockSpec((seq_len, RET_V_W), lambda b: (b, 0)),
    ]
    args = [lg, q, kt, v, rgs]
    if aliased:
        in_specs.append(pl.BlockSpec(memory_space=pl.ANY))
        args.append(states)
    return pl.pallas_call(
        functools.partial(_ret_ctx_kernel, layer=layer, aliased=aliased),
        grid=(nb,),
        in_specs=in_specs,
        out_specs=[
            pl.BlockSpec((seq_len, RET_V_W), lambda b: (b, 0)),
            pl.BlockSpec((None, None, 2, RET_HEADS, RET_DK, RET_DV), lambda b: (b, layer, 0, 0, 0, 0)),
        ],
        out_shape=[
            jax.ShapeDtypeStruct((n, RET_V_W), BF16),
            jax.ShapeDtypeStruct((nb, depth, 2, RET_HEADS, RET_DK, RET_DV), F32),
        ],
        input_output_aliases={5: 1} if aliased else {},
        scratch_shapes=[
            pltpu.VMEM((RET_HEADS, 2, RET_DK, c), BF16),
            pltpu.VMEM((RET_HEADS, c, c), F32),
        ],
        compiler_params=_params(("arbitrary",)),
        name="retention_ctx",
    )(*args)


def _outmlp_kernel(x_ref, an_ref, yg_ref, gab_ref, mod_ref, g2_ref, gf_ref,
                   wco_ref, wro_ref, wo_ref, w1_ref, w2_ref, o_ref, *, final_norm):
    a = _dot(an_ref[...], wco_ref[...])
    b = _dot(yg_ref[...], wro_ref[...])
    m = (gab_ref[:, :D_MODEL].astype(F32) * a + gab_ref[:, D_MODEL:].astype(F32) * b).astype(BF16)
    x = x_ref[...] + mod_ref[2:3, :] * _dot(m, wo_ref[...])
    ms = jnp.mean(x * x, axis=-1, keepdims=True)
    h2 = (x * lax.rsqrt(ms + EPS) * g2_ref[...] * (1.0 + mod_ref[4:5, :]) + mod_ref[3:4, :]).astype(BF16)
    cw = 1024
    acc = None
    for c in range(MLP_W // cw):
        t = jnp.maximum(_dot(h2, w1_ref[:, c * cw:(c + 1) * cw]), 0.0)
        part = _dot((t * t).astype(BF16), w2_ref[c * cw:(c + 1) * cw, :])
        acc = part if acc is None else acc + part
    x = x + mod_ref[5:6, :] * acc
    if final_norm:
        ms = jnp.mean(x * x, axis=-1, keepdims=True)
        x = x * lax.rsqrt(ms + EPS) * gf_ref[...]
    o_ref[...] = x


def _outmlp(x, an, yg, gab, mod, layer, g2, gf, w, seq_len, latent, final_norm):
    n = x.shape[0]
    bt = TOKEN_BLOCK
    per_seq = seq_len // bt
    tok = lambda width: pl.BlockSpec((bt, width), lambda i: (i, 0))
    if latent:
        mod_spec = pl.BlockSpec((None, None, 6, D_MODEL), lambda i: (layer, 1 + i // per_seq, 0, 0))
    else:
        mod_spec = pl.BlockSpec((None, None, 6, D_MODEL), lambda i: (layer, 0, 0, 0))
    return pl.pallas_call(
        functools.partial(_outmlp_kernel, final_norm=final_norm),
        grid=(n // bt,),
        in_specs=[tok(D_MODEL), tok(CONV_CH), tok(RET_V_W), tok(2 * D_MODEL), mod_spec,
                  _layer_resident((1, D_MODEL), layer), _resident((1, D_MODEL))]
                 + [_layer_resident(wi.shape[1:], layer) for wi in w],
        out_specs=tok(D_MODEL),
        out_shape=jax.ShapeDtypeStruct((n, D_MODEL), F32),
        compiler_params=_params(("arbitrary",)),
        name="outmlp_lat" if latent else "outmlp_ctx",
    )(x, an, yg, gab, mod, g2, gf, *w)


def _rope_tables(seq_len):
    t = np.arange(seq_len)
    row = (t // GRID_W).astype(np.float32)
    col = (t % GRID_W).astype(np.float32)
    nf = RET_DK // 4
    inv = (np.float32(ROPE_BASE) ** (-np.arange(nf, dtype=np.float32) / np.float32(nf))).astype(np.float32)
    ang = np.concatenate([row[:, None] * inv, col[:, None] * inv], axis=-1).astype(np.float32)
    cos = np.cos(ang.astype(np.float64)).astype(np.float32)
    sin = np.sin(ang.astype(np.float64)).astype(np.float32)
    return tuple(jnp.asarray(t) for t in (cos, sin, np.ascontiguousarray(cos.T), np.ascontiguousarray(sin.T)))


def kernel(x_prompt, x_sample, c, state_ret, c_ctx, norm1_g, norm2_g, w_mod, b_mod, w_in, conv_dw, conv_b,
           conv_norm_g, w_conv_out, ret_decay_logit, w_ret_out, w_out, w_mlp1, w_mlp2, final_norm_g):
    depth = w_in.shape[0]
    nb_ctx, seq_ctx, _ = x_prompt.shape
    nb_lat, seq_lat, _ = x_sample.shape
    assert 1 + nb_lat <= MOD_ROWS
    assert TOKEN_BLOCK % seq_ctx == 0 and (nb_ctx * seq_ctx) % TOKEN_BLOCK == 0
    assert seq_lat % TOKEN_BLOCK == 0 and TOKEN_BLOCK % GRID_W == 0

    cond = jnp.concatenate([c_ctx[None, :], c, jnp.zeros((MOD_ROWS - 1 - nb_lat, D_MODEL), F32)], axis=0)
    mod, lg = _modulation(cond, w_mod, b_mod, ret_decay_logit.reshape(1, -1))
    mod = mod.reshape(depth, MOD_ROWS, 6, D_MODEL)
    lg = lg.reshape(-1)
    tables = _rope_tables(seq_lat)

    xp = x_prompt.reshape(nb_ctx * seq_ctx, D_MODEL)
    xs = x_sample.reshape(nb_lat * seq_lat, D_MODEL)
    w_in_h = w_in.astype(BF16)
    w_tail = tuple(t.astype(BF16) for t in (w_conv_out, w_ret_out, w_out, w_mlp1, w_mlp2))
    g1 = norm1_g.reshape(depth, 1, D_MODEL)
    g2 = norm2_g.reshape(depth, 1, D_MODEL)
    gf = final_norm_g.reshape(1, D_MODEL)
    nlc = CONV_CH // LANES
    conv = (conv_dw.reshape(depth, CONV_W, nlc, LANES).transpose(0, 2, 1, 3),
            conv_b.reshape(depth, nlc, 1, LANES), conv_norm_g.reshape(depth, nlc, 1, LANES))

    new_state = None
    for l in range(depth):
        last = l == depth - 1
        an, q, kt, v, rgs, gab = _inproj(xp, mod, l, g1, w_in_h, conv, None, seq_ctx, seq_ctx)
        yg, new_state = _retention_ctx(lg, q, kt, v, rgs, new_state, l, depth, seq_ctx)
        xp = _outmlp(xp, an, yg, gab, mod, l, g2, gf, w_tail, seq_ctx, False, last)

        an, q, kt, v, rgs, gab = _inproj(xs, mod, l, g1, w_in_h, conv, tables, seq_lat, GRID_W)
        yg = _retention_lat(lg, q, kt, v, rgs, state_ret, l, seq_lat)
        xs = _outmlp(xs, an, yg, gab, mod, l, g2, gf, w_tail, seq_lat, True, last)

    return (xp.reshape(x_prompt.shape), xs.reshape(x_sample.shape), new_state)
```

```python
import functools

import jax
import jax.numpy as jnp
import numpy as np
from jax import lax
from jax.experimental import pallas as pl
from jax.experimental.pallas import tpu as pltpu

D_MODEL = 1024
GRID_W = 64
CONV_CH = D_MODEL
CONV_W = 31
CONV_PAD = CONV_W // 2
RET_HEADS = 4
RET_DK = 256
RET_DV = 512
RET_QK_W = RET_HEADS * RET_DK
RET_V_W = RET_HEADS * RET_DV
MLP_W = 4 * D_MODEL
ROPE_BASE = 10000.0
EPS = 1e-6

RET_CHUNK = 256
TOKEN_BLOCK = 512
CONV_TILE = 64
CONV_HALO = 16
LANES = 128
SUBLANES = 8
MOD_ROWS = 16
VMEM_LIMIT_BYTES = 58 * 1024 * 1024

BF16 = jnp.bfloat16
F32 = jnp.float32


def _dot(a, b):
    return jnp.dot(a, b, preferred_element_type=F32)


def _resident(shape):
    zeros = (0,) * len(shape)
    return pl.BlockSpec(shape, lambda *_: zeros, pipeline_mode=pl.Buffered(1))


def _layer_resident(shape, layer, *block):
    idx = (layer,) + (block if block else (0,) * len(shape))
    return pl.BlockSpec((None,) + tuple(shape), lambda *_: idx, pipeline_mode=pl.Buffered(1))


def _params(sem):
    return pltpu.CompilerParams(dimension_semantics=sem, vmem_limit_bytes=VMEM_LIMIT_BYTES)


def _mod_kernel(cond_ref, w_ref, b_ref, dec_ref, mod_ref, lg_ref):
    c = cond_ref[...]
    s = (c * jax.nn.sigmoid(c)).astype(BF16)
    mod_ref[...] = _dot(s, w_ref[...].astype(BF16)) + b_ref[...]
    lg_ref[...] = jax.nn.log_sigmoid(dec_ref[...])


def _modulation(cond, w_mod, b_mod, decay):
    depth = w_mod.shape[0]
    nj = w_mod.shape[2] // D_MODEL
    return pl.pallas_call(
        _mod_kernel,
        grid=(depth, nj),
        in_specs=[
            pl.BlockSpec((MOD_ROWS, D_MODEL), lambda l, j: (0, 0)),
            pl.BlockSpec((None, D_MODEL, D_MODEL), lambda l, j: (l, 0, j)),
            pl.BlockSpec((None, 1, D_MODEL), lambda l, j: (l, 0, j)),
            pl.BlockSpec(decay.shape, lambda l, j: (0, 0)),
        ],
        out_specs=[
            pl.BlockSpec((None, MOD_ROWS, D_MODEL), lambda l, j: (l, 0, j)),
            pl.BlockSpec(decay.shape, lambda l, j: (0, 0)),
        ],
        out_shape=[
            jax.ShapeDtypeStruct((depth, MOD_ROWS, nj * D_MODEL), F32),
            jax.ShapeDtypeStruct(decay.shape, F32),
        ],
        compiler_params=_params(("arbitrary", "arbitrary")),
        name="modulation",
    )(cond, w_mod, b_mod.reshape(depth, 1, -1), decay)


def _inproj_kernel(*refs, rope, seg):
    if rope:
        (x_ref, mod_ref, g_ref, cos_ref, sin_ref, cost_ref, sint_ref,
         wu_ref, wq_ref, wk_ref, wv_ref, wrg_ref, wbg_ref, cw_ref, cb_ref, cg_ref,
         an_ref, q_ref, kt_ref, v_ref, rgs_ref, gab_ref, wkt_ref, h_scr, pad_scr, conv_scr) = refs
    else:
        (x_ref, mod_ref, g_ref,
         wu_ref, wq_ref, wk_ref, wv_ref, wrg_ref, wbg_ref, cw_ref, cb_ref, cg_ref,
         an_ref, q_ref, kt_ref, v_ref, rgs_ref, gab_ref, wkt_ref, h_scr, pad_scr, conv_scr) = refs
    bt = x_ref.shape[0]
    nseg = bt // seg
    pseg = seg + 2 * CONV_HALO
    nlc = CONV_CH // LANES

    @pl.when(pl.program_id(0) == 0)
    def _():
        wkt_ref[...] = wk_ref[...].T
        zeros = jnp.zeros((nlc, CONV_HALO, LANES), F32)
        for s in range(nseg):
            pad_scr[:, s * pseg:s * pseg + CONV_HALO, :] = zeros
            pad_scr[:, s * pseg + CONV_HALO + seg:(s + 1) * pseg, :] = zeros

    x = x_ref[...]
    ms = jnp.mean(x * x, axis=-1, keepdims=True)
    y = x * lax.rsqrt(ms + EPS) * g_ref[...]
    h_scr[...] = (y * (1.0 + mod_ref[1:2, :]) + mod_ref[0:1, :]).astype(BF16)

    half = CONV_CH // 2
    for c in range(2):
        u1 = _dot(h_scr[...], wu_ref[:, c * half:(c + 1) * half])
        u2 = _dot(h_scr[...], wu_ref[:, CONV_CH + c * half:CONV_CH + (c + 1) * half])
        a = u1 * jax.nn.sigmoid(u2)
        for p in range(half // LANES):
            lc = c * (half // LANES) + p
            for s in range(nseg):
                pad_scr[lc, s * pseg + CONV_HALO:s * pseg + CONV_HALO + seg, :] = (
                    a[s * seg:(s + 1) * seg, p * LANES:(p + 1) * LANES])

    def conv_tile(t):
        tiles_per_seg = seg // CONV_TILE
        base = (t // tiles_per_seg) * pseg + (t % tiles_per_seg) * CONV_TILE + (CONV_HALO - CONV_PAD)
        ssq = None
        for lc in range(nlc):
            acc = jnp.zeros((CONV_TILE, LANES), F32)
            for tap in range(CONV_W):
                acc = acc + pad_scr[lc, base + tap:base + tap + CONV_TILE, :] * cw_ref[lc, tap:tap + 1, :]
            acc = acc + cb_ref[lc]
            conv_scr[lc] = acc
            ssq = acc * acc if ssq is None else ssq + acc * acc
        scale = lax.rsqrt(jnp.sum(ssq, axis=-1, keepdims=True) * (1.0 / CONV_CH) + EPS)
        for lc in range(nlc):
            cn = conv_scr[lc] * scale * cg_ref[lc]
            an_ref[t * CONV_TILE:(t + 1) * CONV_TILE, lc * LANES:(lc + 1) * LANES] = (
                cn * jax.nn.sigmoid(cn)).astype(BF16)

    hd = RET_DK // 2
    nck = kt_ref.shape[0]
    cw = D_MODEL

    def q_proj():
        for hh in range(RET_HEADS):
            qh = _dot(h_scr[...], wq_ref[:, hh * RET_DK:(hh + 1) * RET_DK])
            if rope:
                x1, x2 = qh[:, :hd], qh[:, hd:]
                cos, sin = cos_ref[...], sin_ref[...]
                q_ref[hh, :, :hd] = (x1 * cos - x2 * sin).astype(BF16)
                q_ref[hh, :, hd:] = (x2 * cos + x1 * sin).astype(BF16)
            else:
                q_ref[hh] = qh.astype(BF16)

    def k_proj():
        for hh in range(RET_HEADS):
            kh = lax.dot_general(wkt_ref[hh * RET_DK:(hh + 1) * RET_DK, :], h_scr[...],
                                 (((1,), (1,)), ((), ())), preferred_element_type=F32)
            kh = kh * (RET_DK ** -0.5)
            if rope:
                x1, x2 = kh[:hd, :], kh[hd:, :]
                cos, sin = cost_ref[...], sint_ref[...]
                kh = jnp.concatenate([x1 * cos - x2 * sin, x2 * cos + x1 * sin], axis=0)
            kh = kh.astype(BF16)
            for j in range(nck):
                kt_ref[j, hh * RET_DK:(hh + 1) * RET_DK, :] = kh[:, j * RET_CHUNK:(j + 1) * RET_CHUNK]

    def v_proj(c):
        sl = slice(c * cw, (c + 1) * cw)
        r = _dot(h_scr[...], wv_ref[:, sl]).astype(BF16)
        for j in range(cw // RET_DV):
            v_ref[c * (cw // RET_DV) + j] = r[:, j * RET_DV:(j + 1) * RET_DV]

    def rg_proj(c):
        sl = slice(c * cw, (c + 1) * cw)
        r = _dot(h_scr[...], wrg_ref[:, sl])
        rgs_ref[:, sl] = (r * jax.nn.sigmoid(r)).astype(BF16)

    def bg_proj(c):
        sl = slice(c * cw, (c + 1) * cw)
        gab_ref[:, sl] = jax.nn.sigmoid(_dot(h_scr[...], wbg_ref[:, sl])).astype(BF16)

    projections = [q_proj, k_proj, functools.partial(v_proj, 0), functools.partial(v_proj, 1),
                   functools.partial(rg_proj, 0), functools.partial(rg_proj, 1),
                   functools.partial(bg_proj, 0), functools.partial(bg_proj, 1)]
    nphase = len(projections)
    ntile = bt // CONV_TILE
    for i, proj in enumerate(projections):
        proj()
        for t in range(i * ntile // nphase, (i + 1) * ntile // nphase):
            conv_tile(t)


def _inproj(x, mod, layer, g, w_in, conv, tables, seq_len, seg):
    n = x.shape[0]
    bt = TOKEN_BLOCK
    rope = tables is not None
    per_seq = seq_len // bt
    nseg = bt // seg
    nlc = CONV_CH // LANES
    tok = lambda width: pl.BlockSpec((bt, width), lambda i: (i, 0))
    if rope:
        mod_spec = pl.BlockSpec((None, None, 6, D_MODEL), lambda i: (layer, 1 + i // per_seq, 0, 0))
    else:
        mod_spec = pl.BlockSpec((None, None, 6, D_MODEL), lambda i: (layer, 0, 0, 0))
    in_specs = [tok(D_MODEL), mod_spec, _layer_resident((1, D_MODEL), layer)]
    args = [x, mod, g]
    if rope:
        cos, sin, cost, sint = tables
        in_specs += [
            pl.BlockSpec((bt, RET_DK // 2), lambda i: (i % per_seq, 0)),
            pl.BlockSpec((bt, RET_DK // 2), lambda i: (i % per_seq, 0)),
            pl.BlockSpec((RET_DK // 2, bt), lambda i: (0, i % per_seq)),
            pl.BlockSpec((RET_DK // 2, bt), lambda i: (0, i % per_seq)),
        ]
        args += [cos, sin, cost, sint]
    wide, narrow = (D_MODEL, 2 * D_MODEL), (D_MODEL, D_MODEL)
    assert 2 * CONV_CH == RET_V_W == 2 * D_MODEL and RET_QK_W == D_MODEL
    in_specs += [
        _layer_resident(wide, layer, 0, 0), _layer_resident(narrow, layer, 0, 2),
        _layer_resident(narrow, layer, 0, 3), _layer_resident(wide, layer, 0, 2),
        _layer_resident(wide, layer, 0, 3), _layer_resident(wide, layer, 0, 4),
    ]
    args += [w_in] * 6
    in_specs += [_layer_resident(t.shape[1:], layer) for t in conv]
    args += list(conv)
    nck = bt // RET_CHUNK
    heads = lambda width: pl.BlockSpec((RET_HEADS, bt, width), lambda i: (0, i, 0))
    out_specs = [
        tok(CONV_CH), heads(RET_DK),
        pl.BlockSpec((nck, RET_QK_W, RET_CHUNK), lambda i: (i, 0, 0)),
        heads(RET_DV), tok(RET_V_W), tok(2 * D_MODEL),
    ]
    out_shape = [
        jax.ShapeDtypeStruct((n, CONV_CH), BF16),
        jax.ShapeDtypeStruct((RET_HEADS, n, RET_DK), BF16),
        jax.ShapeDtypeStruct((n // RET_CHUNK, RET_QK_W, RET_CHUNK), BF16),
        jax.ShapeDtypeStruct((RET_HEADS, n, RET_DV), BF16),
        jax.ShapeDtypeStruct((n, RET_V_W), BF16),
        jax.ShapeDtypeStruct((n, 2 * D_MODEL), BF16),
    ]
    return pl.pallas_call(
        functools.partial(_inproj_kernel, rope=rope, seg=seg),
        grid=(n // bt,),
        in_specs=in_specs,
        out_specs=out_specs,
        out_shape=out_shape,
        scratch_shapes=[
            pltpu.VMEM((RET_QK_W, D_MODEL), BF16),
            pltpu.VMEM((bt, D_MODEL), BF16),
            pltpu.VMEM((nlc, nseg * (seg + 2 * CONV_HALO), LANES), F32),
            pltpu.VMEM((nlc, CONV_TILE, LANES), F32),
        ],
        compiler_params=_params(("arbitrary",)),
        name="inproj_lat" if rope else "inproj_ctx",
    )(*args)


def _decay_tables(lgf, lgb, qd_scr, kd_scr, dm_scr):
    c = RET_CHUNK
    if qd_scr is not None:
        ri = lax.broadcasted_iota(jnp.int32, (c, RET_DK), 0).astype(F32)
        qd_scr[0] = jnp.exp((ri + 1.0) * lgf).astype(BF16)
        qd_scr[1] = jnp.exp((c - ri) * lgb).astype(BF16)
    ci = lax.broadcasted_iota(jnp.int32, (RET_DK, c), 1).astype(F32)
    kd_scr[0] = jnp.exp((c - 1.0 - ci) * lgf).astype(BF16)
    kd_scr[1] = jnp.exp(ci * lgb).astype(BF16)
    di = (lax.broadcasted_iota(jnp.int32, (c, c), 0) - lax.broadcasted_iota(jnp.int32, (c, c), 1)).astype(F32)
    dm_scr[...] = (jnp.where(di >= 0, jnp.exp(jnp.maximum(di, 0.0) * lgf), 0.0)
                   + jnp.where(di <= 0, jnp.exp(jnp.maximum(-di, 0.0) * lgb), 0.0))


def _head_norm(y):
    mu = jnp.mean(y, axis=-1, keepdims=True)
    yc = y - mu
    var = jnp.mean(yc * yc, axis=-1, keepdims=True)
    return (yc * lax.rsqrt(var + EPS)).astype(BF16)


def _ret_lat_kernel(lg_ref, q_ref, kt_ref, v_ref, s0_ref, yn_ref,
                    qd_scr, kd_scr, dm_scr, sd_scr, sf_scr, sb_scr, sfh_scr, sbh_scr, *, layer, nchunk):
    c = RET_CHUNK
    h = pl.program_id(0)

    @pl.when(pl.program_id(1) == 0)
    def _():
        lgf = lg_ref[layer * 2 * RET_HEADS + h]
        lgb = lg_ref[layer * 2 * RET_HEADS + RET_HEADS + h]
        _decay_tables(lgf, lgb, qd_scr, kd_scr, dm_scr)
        sd_scr[0] = jnp.exp(jnp.full((SUBLANES, RET_DV), c * lgf, F32))
        sd_scr[1] = jnp.exp(jnp.full((SUBLANES, RET_DV), c * lgb, F32))

    sdf = sd_scr[0, 0:1, :]
    sdb = sd_scr[1, 0:1, :]

    def rows(i):
        return slice(i * c, (i + 1) * c)

    sb_scr[...] = s0_ref[1]
    sbh_scr[nchunk - 1] = s0_ref[1].astype(BF16)
    for ck in range(nchunk - 1, 0, -1):
        new = sb_scr[...] * sdb + _dot(kt_ref[ck] * kd_scr[1], v_ref[rows(ck), :])
        sbh_scr[ck - 1] = new.astype(BF16)
        if ck > 1:
            sb_scr[...] = new

    sf_scr[...] = s0_ref[0]
    sfh_scr[0] = s0_ref[0].astype(BF16)
    for ck in range(nchunk):
        qc = q_ref[rows(ck), :]
        kc = kt_ref[ck]
        vc = v_ref[rows(ck), :]
        pm = (_dot(qc, kc) * dm_scr[...]).astype(BF16)
        y = _dot(pm, vc) + _dot(qc * qd_scr[0], sfh_scr[ck]) + _dot(qc * qd_scr[1], sbh_scr[ck])
        yn_ref[rows(ck), :] = _head_norm(y)
        if ck < nchunk - 1:
            new = sf_scr[...] * sdf + _dot(kc * kd_scr[0], vc)
            sfh_scr[ck + 1] = new.astype(BF16)
            if ck < nchunk - 2:
                sf_scr[...] = new


def _retention_lat(lg, q, kt, v, state_ret, layer, seq_len):
    n = q.shape[1]
    c = RET_CHUNK
    nchunk = seq_len // c
    return pl.pallas_call(
        functools.partial(_ret_lat_kernel, layer=layer, nchunk=nchunk),
        grid=(RET_HEADS, n // seq_len),
        in_specs=[
            pl.BlockSpec(memory_space=pltpu.SMEM),
            pl.BlockSpec((None, seq_len, RET_DK), lambda h, b: (h, b, 0)),
            pl.BlockSpec((nchunk, RET_DK, c), lambda h, b: (b, h, 0)),
            pl.BlockSpec((None, seq_len, RET_DV), lambda h, b: (h, b, 0)),
            pl.BlockSpec((None, None, 2, None, RET_DK, RET_DV), lambda h, b: (b, layer, 0, h, 0, 0)),
        ],
        out_specs=pl.BlockSpec((None, seq_len, RET_DV), lambda h, b: (h, b, 0)),
        out_shape=jax.ShapeDtypeStruct((RET_HEADS, n, RET_DV), BF16),
        scratch_shapes=[
            pltpu.VMEM((2, c, RET_DK), BF16),
            pltpu.VMEM((2, RET_DK, c), BF16),
            pltpu.VMEM((c, c), F32),
            pltpu.VMEM((2, SUBLANES, RET_DV), F32),
            pltpu.VMEM((RET_DK, RET_DV), F32),
            pltpu.VMEM((RET_DK, RET_DV), F32),
            pltpu.VMEM((nchunk, RET_DK, RET_DV), BF16),
            pltpu.VMEM((nchunk, RET_DK, RET_DV), BF16),
        ],
        compiler_params=_params(("arbitrary", "arbitrary")),
        name="retention_lat",
    )(lg, q, kt, v, state_ret)


def _ret_ctx_kernel(*refs, layer, aliased):
    if aliased:
        refs = refs[:4] + refs[5:]
    lg_ref, q_ref, kt_ref, v_ref, yn_ref, st_ref, kd_scr, dm_scr = refs

    @pl.when(pl.program_id(0) == 0)
    def _():
        for h in range(RET_HEADS):
            lgf = lg_ref[layer * 2 * RET_HEADS + h]
            lgb = lg_ref[layer * 2 * RET_HEADS + RET_HEADS + h]
            _decay_tables(lgf, lgb, None, kd_scr.at[h], dm_scr.at[h])

    for h in range(RET_HEADS):
        qc = q_ref[h]
        kc = kt_ref[0, h * RET_DK:(h + 1) * RET_DK, :]
        vc = v_ref[h]
        pm = (_dot(qc, kc) * dm_scr[h]).astype(BF16)
        yn_ref[h] = _head_norm(_dot(pm, vc))
        st_ref[0, h] = _dot(kc * kd_scr[h, 0], vc)
        st_ref[1, h] = _dot(kc * kd_scr[h, 1], vc)


def _retention_ctx(lg, q, kt, v, states, layer, depth, seq_len):
    c = RET_CHUNK
    assert seq_len == c
    n = q.shape[1]
    nb = n // seq_len
    aliased = states is not None
    in_specs = [
        pl.BlockSpec(memory_space=pltpu.SMEM),
        pl.BlockSpec((RET_HEADS, seq_len, RET_DK), lambda b: (0, b, 0)),
        pl.BlockSpec((1, RET_QK_W, c), lambda b: (b, 0, 0)),
        pl.BlockSpec((RET_HEADS, seq_len, RET_DV), lambda b: (0, b, 0)),
    ]
    args = [lg, q, kt, v]
    if aliased:
        in_specs.append(pl.BlockSpec(memory_space=pl.ANY))
        args.append(states)
    return pl.pallas_call(
        functools.partial(_ret_ctx_kernel, layer=layer, aliased=aliased),
        grid=(nb,),
        in_specs=in_specs,
        out_specs=[
            pl.BlockSpec((RET_HEADS, seq_len, RET_DV), lambda b: (0, b, 0)),
            pl.BlockSpec((None, None, 2, RET_HEADS, RET_DK, RET_DV), lambda b: (b, layer, 0, 0, 0, 0)),
        ],
        out_shape=[
            jax.ShapeDtypeStruct((RET_HEADS, n, RET_DV), BF16),
            jax.ShapeDtypeStruct((nb, depth, 2, RET_HEADS, RET_DK, RET_DV), F32),
        ],
        input_output_aliases={4: 1} if aliased else {},
        scratch_shapes=[
            pltpu.VMEM((RET_HEADS, 2, RET_DK, c), BF16),
            pltpu.VMEM((RET_HEADS, c, c), F32),
        ],
        compiler_params=_params(("arbitrary",)),
        name="retention_ctx",
    )(*args)


def _outmlp_kernel(x_ref, an_ref, yn_ref, rgs_ref, gab_ref, mod_ref, g2_ref, gf_ref,
                   wco_ref, wro_ref, wo_ref, w1_ref, w2_ref, o_ref, *, final_norm):
    a = _dot(an_ref[...], wco_ref[...])
    b = None
    for h in range(RET_HEADS):
        hs = slice(h * RET_DV, (h + 1) * RET_DV)
        part = _dot(rgs_ref[:, hs] * yn_ref[h], wro_ref[hs, :])
        b = part if b is None else b + part
    m = (gab_ref[:, :D_MODEL].astype(F32) * a + gab_ref[:, D_MODEL:].astype(F32) * b).astype(BF16)
    x = x_ref[...] + mod_ref[2:3, :] * _dot(m, wo_ref[...])
    ms = jnp.mean(x * x, axis=-1, keepdims=True)
    h2 = (x * lax.rsqrt(ms + EPS) * g2_ref[...] * (1.0 + mod_ref[4:5, :]) + mod_ref[3:4, :]).astype(BF16)
    cw = 1024
    acc = None
    for c in range(MLP_W // cw):
        t = jnp.maximum(_dot(h2, w1_ref[:, c * cw:(c + 1) * cw]), 0.0)
        part = _dot((t * t).astype(BF16), w2_ref[c * cw:(c + 1) * cw, :])
        acc = part if acc is None else acc + part
    x = x + mod_ref[5:6, :] * acc
    if final_norm:
        ms = jnp.mean(x * x, axis=-1, keepdims=True)
        x = x * lax.rsqrt(ms + EPS) * gf_ref[...]
    o_ref[...] = x


def _outmlp(x, an, yn, rgs, gab, mod, layer, g2, gf, w, seq_len, latent, final_norm):
    n = x.shape[0]
    bt = TOKEN_BLOCK
    per_seq = seq_len // bt
    tok = lambda width: pl.BlockSpec((bt, width), lambda i: (i, 0))
    if latent:
        mod_spec = pl.BlockSpec((None, None, 6, D_MODEL), lambda i: (layer, 1 + i // per_seq, 0, 0))
    else:
        mod_spec = pl.BlockSpec((None, None, 6, D_MODEL), lambda i: (layer, 0, 0, 0))
    return pl.pallas_call(
        functools.partial(_outmlp_kernel, final_norm=final_norm),
        grid=(n // bt,),
        in_specs=[tok(D_MODEL), tok(CONV_CH), pl.BlockSpec((RET_HEADS, bt, RET_DV), lambda i: (0, i, 0)),
                  tok(RET_V_W), tok(2 * D_MODEL), mod_spec,
                  _layer_resident((1, D_MODEL), layer), _resident((1, D_MODEL))]
                 + [_layer_resident(wi.shape[1:], layer) for wi in w],
        out_specs=tok(D_MODEL),
        out_shape=jax.ShapeDtypeStruct((n, D_MODEL), F32),
        compiler_params=_params(("arbitrary",)),
        name="outmlp_lat" if latent else "outmlp_ctx",
    )(x, an, yn, rgs, gab, mod, g2, gf, *w)


def _rope_tables(seq_len):
    t = np.arange(seq_len)
    row = (t // GRID_W).astype(np.float32)
    col = (t % GRID_W).astype(np.float32)
    nf = RET_DK // 4
    inv = (np.float32(ROPE_BASE) ** (-np.arange(nf, dtype=np.float32) / np.float32(nf))).astype(np.float32)
    ang = np.concatenate([row[:, None] * inv, col[:, None] * inv], axis=-1).astype(np.float32)
    cos = np.cos(ang.astype(np.float64)).astype(np.float32)
    sin = np.sin(ang.astype(np.float64)).astype(np.float32)
    return tuple(jnp.asarray(t) for t in (cos, sin, np.ascontiguousarray(cos.T), np.ascontiguousarray(sin.T)))


def kernel(x_prompt, x_sample, c, state_ret, c_ctx, norm1_g, norm2_g, w_mod, b_mod, w_in, conv_dw, conv_b,
           conv_norm_g, w_conv_out, ret_decay_logit, w_ret_out, w_out, w_mlp1, w_mlp2, final_norm_g):
    depth = w_in.shape[0]
    nb_ctx, seq_ctx, _ = x_prompt.shape
    nb_lat, seq_lat, _ = x_sample.shape
    assert 1 + nb_lat <= MOD_ROWS
    assert TOKEN_BLOCK % seq_ctx == 0 and (nb_ctx * seq_ctx) % TOKEN_BLOCK == 0
    assert seq_lat % TOKEN_BLOCK == 0 and TOKEN_BLOCK % GRID_W == 0

    cond = jnp.concatenate([c_ctx[None, :], c, jnp.zeros((MOD_ROWS - 1 - nb_lat, D_MODEL), F32)], axis=0)
    mod, lg = _modulation(cond, w_mod, b_mod, ret_decay_logit.reshape(1, -1))
    mod = mod.reshape(depth, MOD_ROWS, 6, D_MODEL)
    lg = lg.reshape(-1)
    tables = _rope_tables(seq_lat)

    xp = x_prompt.reshape(nb_ctx * seq_ctx, D_MODEL)
    xs = x_sample.reshape(nb_lat * seq_lat, D_MODEL)
    w_in_h = w_in.astype(BF16)
    w_tail = tuple(t.astype(BF16) for t in (w_conv_out, w_ret_out, w_out, w_mlp1, w_mlp2))
    g1 = norm1_g.reshape(depth, 1, D_MODEL)
    g2 = norm2_g.reshape(depth, 1, D_MODEL)
    gf = final_norm_g.reshape(1, D_MODEL)
    nlc = CONV_CH // LANES
    conv = (conv_dw.reshape(depth, CONV_W, nlc, LANES).transpose(0, 2, 1, 3),
            conv_b.reshape(depth, nlc, 1, LANES), conv_norm_g.reshape(depth, nlc, 1, LANES))

    new_state = None
    for l in range(depth):
        last = l == depth - 1
        an, q, kt, v, rgs, gab = _inproj(xp, mod, l, g1, w_in_h, conv, None, seq_ctx, seq_ctx)
        yn, new_state = _retention_ctx(lg, q, kt, v, new_state, l, depth, seq_ctx)
        xp = _outmlp(xp, an, yn, rgs, gab, mod, l, g2, gf, w_tail, seq_ctx, False, last)

        an, q, kt, v, rgs, gab = _inproj(xs, mod, l, g1, w_in_h, conv, tables, seq_lat, GRID_W)
        yn = _retention_lat(lg, q, kt, v, state_ret, l, seq_lat)
        xs = _outmlp(xs, an, yn, rgs, gab, mod, l, g2, gf, w_tail, seq_lat, True, last)

    return (xp.reshape(x_prompt.shape), xs.reshape(x_sample.shape), new_state)
```

```python
import functools

import jax
import jax.numpy as jnp
import numpy as np
from jax import lax
from jax.experimental import pallas as pl
from jax.experimental.pallas import tpu as pltpu

D_MODEL = 1024
GRID_W = 64
CONV_CH = D_MODEL
CONV_W = 31
CONV_PAD = CONV_W // 2
RET_HEADS = 4
RET_DK = 256
RET_DV = 512
RET_QK_W = RET_HEADS * RET_DK
RET_V_W = RET_HEADS * RET_DV
MLP_W = 4 * D_MODEL
ROPE_BASE = 10000.0
EPS = 1e-6

RET_CHUNK = 256
TOKEN_BLOCK = 512
CONV_TILE = 64
CONV_HALO = 16
LANES = 128
SUBLANES = 8
MOD_ROWS = 16
VMEM_LIMIT_BYTES = 58 * 1024 * 1024

BF16 = jnp.bfloat16
F32 = jnp.float32


def _dot(a, b):
    return jnp.dot(a, b, preferred_element_type=F32)


def _resident(shape):
    zeros = (0,) * len(shape)
    return pl.BlockSpec(shape, lambda *_: zeros, pipeline_mode=pl.Buffered(1))


def _layer_resident(shape, layer, *block):
    idx = (layer,) + (block if block else (0,) * len(shape))
    return pl.BlockSpec((None,) + tuple(shape), lambda *_: idx, pipeline_mode=pl.Buffered(1))


def _params(sem):
    return pltpu.CompilerParams(dimension_semantics=sem, vmem_limit_bytes=VMEM_LIMIT_BYTES)


def _mod_kernel(cond_ref, w_ref, b_ref, dec_ref, mod_ref, lg_ref):
    c = cond_ref[...]
    s = (c * jax.nn.sigmoid(c)).astype(BF16)
    mod_ref[...] = _dot(s, w_ref[...].astype(BF16)) + b_ref[...]
    lg_ref[...] = jax.nn.log_sigmoid(dec_ref[...])


def _modulation(cond, w_mod, b_mod, decay):
    depth = w_mod.shape[0]
    nj = w_mod.shape[2] // D_MODEL
    return pl.pallas_call(
        _mod_kernel,
        grid=(depth, nj),
        in_specs=[
            pl.BlockSpec((MOD_ROWS, D_MODEL), lambda l, j: (0, 0)),
            pl.BlockSpec((None, D_MODEL, D_MODEL), lambda l, j: (l, 0, j)),
            pl.BlockSpec((None, 1, D_MODEL), lambda l, j: (l, 0, j)),
            pl.BlockSpec(decay.shape, lambda l, j: (0, 0)),
        ],
        out_specs=[
            pl.BlockSpec((None, MOD_ROWS, D_MODEL), lambda l, j: (l, 0, j)),
            pl.BlockSpec(decay.shape, lambda l, j: (0, 0)),
        ],
        out_shape=[
            jax.ShapeDtypeStruct((depth, MOD_ROWS, nj * D_MODEL), F32),
            jax.ShapeDtypeStruct(decay.shape, F32),
        ],
        compiler_params=_params(("arbitrary", "arbitrary")),
        name="modulation",
    )(cond, w_mod, b_mod.reshape(depth, 1, -1), decay)


def _inproj_kernel(*refs, rope, seg):
    if rope:
        (x_ref, mod_ref, g_ref, cos_ref, sin_ref, cost_ref, sint_ref,
         wu_ref, wq_ref, wk_ref, wv_ref, wrg_ref, wbg_ref, cw_ref, cb_ref, cg_ref,
         an_ref, q_ref, kt_ref, v_ref, rgs_ref, gab_ref, wkt_ref, h_scr, pad_scr, conv_scr) = refs
    else:
        (x_ref, mod_ref, g_ref,
         wu_ref, wq_ref, wk_ref, wv_ref, wrg_ref, wbg_ref, cw_ref, cb_ref, cg_ref,
         an_ref, q_ref, kt_ref, v_ref, rgs_ref, gab_ref, wkt_ref, h_scr, pad_scr, conv_scr) = refs
    bt = x_ref.shape[0]
    nseg = bt // seg
    pseg = seg + 2 * CONV_HALO
    nlc = CONV_CH // LANES

    @pl.when(pl.program_id(0) == 0)
    def _():
        wkt_ref[...] = wk_ref[...].T
        zeros = jnp.zeros((nlc, CONV_HALO, LANES), F32)
        for s in range(nseg):
            pad_scr[:, s * pseg:s * pseg + CONV_HALO, :] = zeros
            pad_scr[:, s * pseg + CONV_HALO + seg:(s + 1) * pseg, :] = zeros

    x = x_ref[...]
    ms = jnp.mean(x * x, axis=-1, keepdims=True)
    y = x * lax.rsqrt(ms + EPS) * g_ref[...]
    h_scr[...] = (y * (1.0 + mod_ref[1:2, :]) + mod_ref[0:1, :]).astype(BF16)

    ntile = bt // CONV_TILE
    tiles_per_seg = seg // CONV_TILE
    glu_w = 2 * LANES

    def glu_piece(p):
        u1 = _dot(h_scr[...], wu_ref[:, p * glu_w:(p + 1) * glu_w])
        u2 = _dot(h_scr[...], wu_ref[:, CONV_CH + p * glu_w:CONV_CH + (p + 1) * glu_w])
        a = u1 * jax.nn.sigmoid(u2)
        for j in range(glu_w // LANES):
            lc = p * (glu_w // LANES) + j
            for s in range(nseg):
                pad_scr[lc, s * pseg + CONV_HALO:s * pseg + CONV_HALO + seg, :] = (
                    a[s * seg:(s + 1) * seg, j * LANES:(j + 1) * LANES])

    def conv_chunk(lc):
        for t in range(ntile):
            base = (t // tiles_per_seg) * pseg + (t % tiles_per_seg) * CONV_TILE + (CONV_HALO - CONV_PAD)
            acc = jnp.zeros((CONV_TILE, LANES), F32)
            for tap in range(CONV_W):
                acc = acc + pad_scr[lc, base + tap:base + tap + CONV_TILE, :] * cw_ref[lc, tap:tap + 1, :]
            conv_scr[lc, t * CONV_TILE:(t + 1) * CONV_TILE, :] = acc + cb_ref[lc]

    def conv_finish(t):
        rows = slice(t * CONV_TILE, (t + 1) * CONV_TILE)
        ssq = None
        for lc in range(nlc):
            cv = conv_scr[lc, rows, :]
            ssq = cv * cv if ssq is None else ssq + cv * cv
        scale = lax.rsqrt(jnp.sum(ssq, axis=-1, keepdims=True) * (1.0 / CONV_CH) + EPS)
        for lc in range(nlc):
            cn = conv_scr[lc, rows, :] * scale * cg_ref[lc]
            an_ref[rows, lc * LANES:(lc + 1) * LANES] = (cn * jax.nn.sigmoid(cn)).astype(BF16)

    hd = RET_DK // 2
    nck = kt_ref.shape[0]
    cw = D_MODEL

    def q_proj():
        for hh in range(RET_HEADS):
            qh = _dot(h_scr[...], wq_ref[:, hh * RET_DK:(hh + 1) * RET_DK])
            if rope:
                x1, x2 = qh[:, :hd], qh[:, hd:]
                cos, sin = cos_ref[...], sin_ref[...]
                q_ref[hh, :, :hd] = (x1 * cos - x2 * sin).astype(BF16)
                q_ref[hh, :, hd:] = (x2 * cos + x1 * sin).astype(BF16)
            else:
                q_ref[hh] = qh.astype(BF16)

    def k_proj():
        for hh in range(RET_HEADS):
            kh = lax.dot_general(wkt_ref[hh * RET_DK:(hh + 1) * RET_DK, :], h_scr[...],
                                 (((1,), (1,)), ((), ())), preferred_element_type=F32)
            kh = kh * (RET_DK ** -0.5)
            if rope:
                x1, x2 = kh[:hd, :], kh[hd:, :]
                cos, sin = cost_ref[...], sint_ref[...]
                kh = jnp.concatenate([x1 * cos - x2 * sin, x2 * cos + x1 * sin], axis=0)
            kh = kh.astype(BF16)
            for j in range(nck):
                kt_ref[j, hh * RET_DK:(hh + 1) * RET_DK, :] = kh[:, j * RET_CHUNK:(j + 1) * RET_CHUNK]

    def v_proj(c):
        sl = slice(c * cw, (c + 1) * cw)
        r = _dot(h_scr[...], wv_ref[:, sl]).astype(BF16)
        for j in range(cw // RET_DV):
            v_ref[c * (cw // RET_DV) + j] = r[:, j * RET_DV:(j + 1) * RET_DV]

    def rg_proj(c):
        sl = slice(c * cw, (c + 1) * cw)
        r = _dot(h_scr[...], wrg_ref[:, sl])
        rgs_ref[:, sl] = (r * jax.nn.sigmoid(r)).astype(BF16)

    def bg_proj(c):
        sl = slice(c * cw, (c + 1) * cw)
        gab_ref[:, sl] = jax.nn.sigmoid(_dot(h_scr[...], wbg_ref[:, sl])).astype(BF16)

    assert CONV_CH // glu_w == 4 and nlc == 8
    glu_piece(0); q_proj(); conv_chunk(0); conv_chunk(1)
    glu_piece(1); k_proj(); conv_chunk(2); conv_chunk(3)
    glu_piece(2); v_proj(0); glu_piece(3); v_proj(1)
    rg_proj(0); conv_chunk(4); conv_chunk(5); rg_proj(1); conv_chunk(6); conv_chunk(7)
    for t in range(0, ntile // 2):
        conv_finish(t)
    bg_proj(0)
    for t in range(ntile // 2, ntile):
        conv_finish(t)
    bg_proj(1)


def _inproj(x, mod, layer, g, w_in, conv, tables, seq_len, seg):
    n = x.shape[0]
    bt = TOKEN_BLOCK
    rope = tables is not None
    per_seq = seq_len // bt
    nseg = bt // seg
    nlc = CONV_CH // LANES
    tok = lambda width: pl.BlockSpec((bt, width), lambda i: (i, 0))
    if rope:
        mod_spec = pl.BlockSpec((None, None, 6, D_MODEL), lambda i: (layer, 1 + i // per_seq, 0, 0))
    else:
        mod_spec = pl.BlockSpec((None, None, 6, D_MODEL), lambda i: (layer, 0, 0, 0))
    in_specs = [tok(D_MODEL), mod_spec, _layer_resident((1, D_MODEL), layer)]
    args = [x, mod, g]
    if rope:
        cos, sin, cost, sint = tables
        in_specs += [
            pl.BlockSpec((bt, RET_DK // 2), lambda i: (i % per_seq, 0)),
            pl.BlockSpec((bt, RET_DK // 2), lambda i: (i % per_seq, 0)),
            pl.BlockSpec((RET_DK // 2, bt), lambda i: (0, i % per_seq)),
            pl.BlockSpec((RET_DK // 2, bt), lambda i: (0, i % per_seq)),
        ]
        args += [cos, sin, cost, sint]
    wide, narrow = (D_MODEL, 2 * D_MODEL), (D_MODEL, D_MODEL)
    assert 2 * CONV_CH == RET_V_W == 2 * D_MODEL and RET_QK_W == D_MODEL
    in_specs += [
        _layer_resident(wide, layer, 0, 0), _layer_resident(narrow, layer, 0, 2),
        _layer_resident(narrow, layer, 0, 3), _layer_resident(wide, layer, 0, 2),
        _layer_resident(wide, layer, 0, 3), _layer_resident(wide, layer, 0, 4),
    ]
    args += [w_in] * 6
    in_specs += [_layer_resident(t.shape[1:], layer) for t in conv]
    args += list(conv)
    nck = bt // RET_CHUNK
    heads = lambda width: pl.BlockSpec((RET_HEADS, bt, width), lambda i: (0, i, 0))
    out_specs = [
        tok(CONV_CH), heads(RET_DK),
        pl.BlockSpec((nck, RET_QK_W, RET_CHUNK), lambda i: (i, 0, 0)),
        heads(RET_DV), tok(RET_V_W), tok(2 * D_MODEL),
    ]
    out_shape = [
        jax.ShapeDtypeStruct((n, CONV_CH), BF16),
        jax.ShapeDtypeStruct((RET_HEADS, n, RET_DK), BF16),
        jax.ShapeDtypeStruct((n // RET_CHUNK, RET_QK_W, RET_CHUNK), BF16),
        jax.ShapeDtypeStruct((RET_HEADS, n, RET_DV), BF16),
        jax.ShapeDtypeStruct((n, RET_V_W), BF16),
        jax.ShapeDtypeStruct((n, 2 * D_MODEL), BF16),
    ]
    return pl.pallas_call(
        functools.partial(_inproj_kernel, rope=rope, seg=seg),
        grid=(n // bt,),
        in_specs=in_specs,
        out_specs=out_specs,
        out_shape=out_shape,
        scratch_shapes=[
            pltpu.VMEM((RET_QK_W, D_MODEL), BF16),
            pltpu.VMEM((bt, D_MODEL), BF16),
            pltpu.VMEM((nlc, nseg * (seg + 2 * CONV_HALO), LANES), F32),
            pltpu.VMEM((nlc, bt, LANES), F32),
        ],
        compiler_params=_params(("arbitrary",)),
        name="inproj_lat" if rope else "inproj_ctx",
    )(*args)


def _decay_tables(lgf, lgb, qd_scr, kd_scr, dm_scr):
    c = RET_CHUNK
    if qd_scr is not None:
        ri = lax.broadcasted_iota(jnp.int32, (c, RET_DK), 0).astype(F32)
        qd_scr[0] = jnp.exp((ri + 1.0) * lgf).astype(BF16)
        qd_scr[1] = jnp.exp((c - ri) * lgb).astype(BF16)
    ci = lax.broadcasted_iota(jnp.int32, (RET_DK, c), 1).astype(F32)
    kd_scr[0] = jnp.exp((c - 1.0 - ci) * lgf).astype(BF16)
    kd_scr[1] = jnp.exp(ci * lgb).astype(BF16)
    di = (lax.broadcasted_iota(jnp.int32, (c, c), 0) - lax.broadcasted_iota(jnp.int32, (c, c), 1)).astype(F32)
    dm_scr[...] = (jnp.where(di >= 0, jnp.exp(jnp.maximum(di, 0.0) * lgf), 0.0)
                   + jnp.where(di <= 0, jnp.exp(jnp.maximum(-di, 0.0) * lgb), 0.0))


def _head_norm(y):
    mu = jnp.mean(y, axis=-1, keepdims=True)
    yc = y - mu
    var = jnp.mean(yc * yc, axis=-1, keepdims=True)
    return (yc * lax.rsqrt(var + EPS)).astype(BF16)


def _ret_lat_kernel(lg_ref, q_ref, kt_ref, v_ref, s0_ref, yn_ref,
                    qd_scr, kd_scr, dm_scr, sd_scr, sf_scr, sb_scr, sfh_scr, sbh_scr, *, layer, nchunk):
    c = RET_CHUNK
    h = pl.program_id(0)

    @pl.when(pl.program_id(1) == 0)
    def _():
        lgf = lg_ref[layer * 2 * RET_HEADS + h]
        lgb = lg_ref[layer * 2 * RET_HEADS + RET_HEADS + h]
        _decay_tables(lgf, lgb, qd_scr, kd_scr, dm_scr)
        sd_scr[0] = jnp.exp(jnp.full((SUBLANES, RET_DV), c * lgf, F32))
        sd_scr[1] = jnp.exp(jnp.full((SUBLANES, RET_DV), c * lgb, F32))

    sdf = sd_scr[0, 0:1, :]
    sdb = sd_scr[1, 0:1, :]

    def rows(i):
        return slice(i * c, (i + 1) * c)

    sb_scr[...] = s0_ref[1]
    sbh_scr[nchunk - 1] = s0_ref[1].astype(BF16)
    for ck in range(nchunk - 1, 0, -1):
        new = sb_scr[...] * sdb + _dot(kt_ref[ck] * kd_scr[1], v_ref[rows(ck), :])
        sbh_scr[ck - 1] = new.astype(BF16)
        if ck > 1:
            sb_scr[...] = new

    sf_scr[...] = s0_ref[0]
    sfh_scr[0] = s0_ref[0].astype(BF16)
    for ck in range(nchunk):
        qc = q_ref[rows(ck), :]
        kc = kt_ref[ck]
        vc = v_ref[rows(ck), :]
        pm = (_dot(qc, kc) * dm_scr[...]).astype(BF16)
        y = _dot(pm, vc) + _dot(qc * qd_scr[0], sfh_scr[ck]) + _dot(qc * qd_scr[1], sbh_scr[ck])
        yn_ref[rows(ck), :] = _head_norm(y)
        if ck < nchunk - 1:
            new = sf_scr[...] * sdf + _dot(kc * kd_scr[0], vc)
            sfh_scr[ck + 1] = new.astype(BF16)
            if ck < nchunk - 2:
                sf_scr[...] = new


def _retention_lat(lg, q, kt, v, state_ret, layer, seq_len):
    n = q.shape[1]
    c = RET_CHUNK
    nchunk = seq_len // c
    return pl.pallas_call(
        functools.partial(_ret_lat_kernel, layer=layer, nchunk=nchunk),
        grid=(RET_HEADS, n // seq_len),
        in_specs=[
            pl.BlockSpec(memory_space=pltpu.SMEM),
            pl.BlockSpec((None, seq_len, RET_DK), lambda h, b: (h, b, 0)),
            pl.BlockSpec((nchunk, RET_DK, c), lambda h, b: (b, h, 0)),
            pl.BlockSpec((None, seq_len, RET_DV), lambda h, b: (h, b, 0)),
            pl.BlockSpec((None, None, 2, None, RET_DK, RET_DV), lambda h, b: (b, layer, 0, h, 0, 0)),
        ],
        out_specs=pl.BlockSpec((None, seq_len, RET_DV), lambda h, b: (h, b, 0)),
        out_shape=jax.ShapeDtypeStruct((RET_HEADS, n, RET_DV), BF16),
        scratch_shapes=[
            pltpu.VMEM((2, c, RET_DK), BF16),
            pltpu.VMEM((2, RET_DK, c), BF16),
            pltpu.VMEM((c, c), F32),
            pltpu.VMEM((2, SUBLANES, RET_DV), F32),
            pltpu.VMEM((RET_DK, RET_DV), F32),
            pltpu.VMEM((RET_DK, RET_DV), F32),
            pltpu.VMEM((nchunk, RET_DK, RET_DV), BF16),
            pltpu.VMEM((nchunk, RET_DK, RET_DV), BF16),
        ],
        compiler_params=_params(("arbitrary", "arbitrary")),
        name="retention_lat",
    )(lg, q, kt, v, state_ret)


def _ret_ctx_kernel(*refs, layer, aliased):
    if aliased:
        refs = refs[:4] + refs[5:]
    lg_ref, q_ref, kt_ref, v_ref, yn_ref, st_ref, kd_scr, dm_scr = refs

    @pl.when(pl.program_id(0) == 0)
    def _():
        for h in range(RET_HEADS):
            lgf = lg_ref[layer * 2 * RET_HEADS + h]
            lgb = lg_ref[layer * 2 * RET_HEADS + RET_HEADS + h]
            _decay_tables(lgf, lgb, None, kd_scr.at[h], dm_scr.at[h])

    for h in range(RET_HEADS):
        qc = q_ref[h]
        kc = kt_ref[0, h * RET_DK:(h + 1) * RET_DK, :]
        vc = v_ref[h]
        pm = (_dot(qc, kc) * dm_scr[h]).astype(BF16)
        yn_ref[h] = _head_norm(_dot(pm, vc))
        st_ref[0, h] = _dot(kc * kd_scr[h, 0], vc)
        st_ref[1, h] = _dot(kc * kd_scr[h, 1], vc)


def _retention_ctx(lg, q, kt, v, states, layer, depth, seq_len):
    c = RET_CHUNK
    assert seq_len == c
    n = q.shape[1]
    nb = n // seq_len
    aliased = states is not None
    in_specs = [
        pl.BlockSpec(memory_space=pltpu.SMEM),
        pl.BlockSpec((RET_HEADS, seq_len, RET_DK), lambda b: (0, b, 0)),
        pl.BlockSpec((1, RET_QK_W, c), lambda b: (b, 0, 0)),
        pl.BlockSpec((RET_HEADS, seq_len, RET_DV), lambda b: (0, b, 0)),
    ]
    args = [lg, q, kt, v]
    if aliased:
        in_specs.append(pl.BlockSpec(memory_space=pl.ANY))
        args.append(states)
    return pl.pallas_call(
        functools.partial(_ret_ctx_kernel, layer=layer, aliased=aliased),
        grid=(nb,),
        in_specs=in_specs,
        out_specs=[
            pl.BlockSpec((RET_HEADS, seq_len, RET_DV), lambda b: (0, b, 0)),
            pl.BlockSpec((None, None, 2, RET_HEADS, RET_DK, RET_DV), lambda b: (b, layer, 0, 0, 0, 0)),
        ],
        out_shape=[
            jax.ShapeDtypeStruct((RET_HEADS, n, RET_DV), BF16),
            jax.ShapeDtypeStruct((nb, depth, 2, RET_HEADS, RET_DK, RET_DV), F32),
        ],
        input_output_aliases={4: 1} if aliased else {},
        scratch_shapes=[
            pltpu.VMEM((RET_HEADS, 2, RET_DK, c), BF16),
            pltpu.VMEM((RET_HEADS, c, c), F32),
        ],
        compiler_params=_params(("arbitrary",)),
        name="retention_ctx",
    )(*args)


def _outmlp_kernel(x_ref, an_ref, yn_ref, rgs_ref, gab_ref, mod_ref, g2_ref, gf_ref,
                   wco_ref, wro_ref, wo_ref, w1_ref, w2_ref, o_ref, *, final_norm):
    a = _dot(an_ref[...], wco_ref[...])
    b = None
    for h in range(RET_HEADS):
        hs = slice(h * RET_DV, (h + 1) * RET_DV)
        part = _dot(rgs_ref[:, hs] * yn_ref[h], wro_ref[hs, :])
        b = part if b is None else b + part
    m = (gab_ref[:, :D_MODEL].astype(F32) * a + gab_ref[:, D_MODEL:].astype(F32) * b).astype(BF16)
    x = x_ref[...] + mod_ref[2:3, :] * _dot(m, wo_ref[...])
    ms = jnp.mean(x * x, axis=-1, keepdims=True)
    h2 = (x * lax.rsqrt(ms + EPS) * g2_ref[...] * (1.0 + mod_ref[4:5, :]) + mod_ref[3:4, :]).astype(BF16)
    cw = 1024
    acc = None
    for c in range(MLP_W // cw):
        t = jnp.maximum(_dot(h2, w1_ref[:, c * cw:(c + 1) * cw]), 0.0)
        part = _dot((t * t).astype(BF16), w2_ref[c * cw:(c + 1) * cw, :])
        acc = part if acc is None else acc + part
    x = x + mod_ref[5:6, :] * acc
    if final_norm:
        ms = jnp.mean(x * x, axis=-1, keepdims=True)
        x = x * lax.rsqrt(ms + EPS) * gf_ref[...]
    o_ref[...] = x


def _outmlp(x, an, yn, rgs, gab, mod, layer, g2, gf, w, seq_len, latent, final_norm):
    n = x.shape[0]
    bt = TOKEN_BLOCK
    per_seq = seq_len // bt
    tok = lambda width: pl.BlockSpec((bt, width), lambda i: (i, 0))
    if latent:
        mod_spec = pl.BlockSpec((None, None, 6, D_MODEL), lambda i: (layer, 1 + i // per_seq, 0, 0))
    else:
        mod_spec = pl.BlockSpec((None, None, 6, D_MODEL), lambda i: (layer, 0, 0, 0))
    return pl.pallas_call(
        functools.partial(_outmlp_kernel, final_norm=final_norm),
        grid=(n // bt,),
        in_specs=[tok(D_MODEL), tok(CONV_CH), pl.BlockSpec((RET_HEADS, bt, RET_DV), lambda i: (0, i, 0)),
                  tok(RET_V_W), tok(2 * D_MODEL), mod_spec,
                  _layer_resident((1, D_MODEL), layer), _resident((1, D_MODEL))]
                 + [_layer_resident(wi.shape[1:], layer) for wi in w],
        out_specs=tok(D_MODEL),
        out_shape=jax.ShapeDtypeStruct((n, D_MODEL), F32),
        compiler_params=_params(("arbitrary",)),
        name="outmlp_lat" if latent else "outmlp_ctx",
    )(x, an, yn, rgs, gab, mod, g2, gf, *w)


def _rope_tables(seq_len):
    t = np.arange(seq_len)
    row = (t // GRID_W).astype(np.float32)
    col = (t % GRID_W).astype(np.float32)
    nf = RET_DK // 4
    inv = (np.float32(ROPE_BASE) ** (-np.arange(nf, dtype=np.float32) / np.float32(nf))).astype(np.float32)
    ang = np.concatenate([row[:, None] * inv, col[:, None] * inv], axis=-1).astype(np.float32)
    cos = np.cos(ang.astype(np.float64)).astype(np.float32)
    sin = np.sin(ang.astype(np.float64)).astype(np.float32)
    return tuple(jnp.asarray(t) for t in (cos, sin, np.ascontiguousarray(cos.T), np.ascontiguousarray(sin.T)))


def kernel(x_prompt, x_sample, c, state_ret, c_ctx, norm1_g, norm2_g, w_mod, b_mod, w_in, conv_dw, conv_b,
           conv_norm_g, w_conv_out, ret_decay_logit, w_ret_out, w_out, w_mlp1, w_mlp2, final_norm_g):
    depth = w_in.shape[0]
    nb_ctx, seq_ctx, _ = x_prompt.shape
    nb_lat, seq_lat, _ = x_sample.shape
    assert 1 + nb_lat <= MOD_ROWS
    assert TOKEN_BLOCK % seq_ctx == 0 and (nb_ctx * seq_ctx) % TOKEN_BLOCK == 0
    assert seq_lat % TOKEN_BLOCK == 0 and TOKEN_BLOCK % GRID_W == 0

    cond = jnp.concatenate([c_ctx[None, :], c, jnp.zeros((MOD_ROWS - 1 - nb_lat, D_MODEL), F32)], axis=0)
    mod, lg = _modulation(cond, w_mod, b_mod, ret_decay_logit.reshape(1, -1))
    mod = mod.reshape(depth, MOD_ROWS, 6, D_MODEL)
    lg = lg.reshape(-1)
    tables = _rope_tables(seq_lat)

    xp = x_prompt.reshape(nb_ctx * seq_ctx, D_MODEL)
    xs = x_sample.reshape(nb_lat * seq_lat, D_MODEL)
    w_in_h = w_in.astype(BF16)
    w_tail = tuple(t.astype(BF16) for t in (w_conv_out, w_ret_out, w_out, w_mlp1, w_mlp2))
    g1 = norm1_g.reshape(depth, 1, D_MODEL)
    g2 = norm2_g.reshape(depth, 1, D_MODEL)
    gf = final_norm_g.reshape(1, D_MODEL)
    nlc = CONV_CH // LANES
    conv = (conv_dw.reshape(depth, CONV_W, nlc, LANES).transpose(0, 2, 1, 3),
            conv_b.reshape(depth, nlc, 1, LANES), conv_norm_g.reshape(depth, nlc, 1, LANES))

    new_state = None
    for l in range(depth):
        last = l == depth - 1
        an, q, kt, v, rgs, gab = _inproj(xp, mod, l, g1, w_in_h, conv, None, seq_ctx, seq_ctx)
        yn, new_state = _retention_ctx(lg, q, kt, v, new_state, l, depth, seq_ctx)
        xp = _outmlp(xp, an, yn, rgs, gab, mod, l, g2, gf, w_tail, seq_ctx, False, last)

        an, q, kt, v, rgs, gab = _inproj(xs, mod, l, g1, w_in_h, conv, tables, seq_lat, GRID_W)
        yn = _retention_lat(lg, q, kt, v, state_ret, l, seq_lat)
        xs = _outmlp(xs, an, yn, rgs, gab, mod, l, g2, gf, w_tail, seq_lat, True, last)

    return (xp.reshape(x_prompt.shape), xs.reshape(x_sample.shape), new_state)
```

```python
import functools

import jax
import jax.numpy as jnp
import numpy as np
from jax import lax
from jax.experimental import pallas as pl
from jax.experimental.pallas import tpu as pltpu

D_MODEL = 1024
GRID_W = 64
CONV_CH = D_MODEL
CONV_W = 31
CONV_PAD = CONV_W // 2
RET_HEADS = 4
RET_DK = 256
RET_DV = 512
RET_QK_W = RET_HEADS * RET_DK
RET_V_W = RET_HEADS * RET_DV
MLP_W = 4 * D_MODEL
ROPE_BASE = 10000.0
EPS = 1e-6

RET_CHUNK = 256
TOKEN_BLOCK = 512
CONV_TILE = 64
CONV_HALO = 16
LANES = 128
SUBLANES = 8
MOD_ROWS = 16
VMEM_LIMIT_BYTES = 58 * 1024 * 1024

BF16 = jnp.bfloat16
F32 = jnp.float32


def _dot(a, b):
    return jnp.dot(a, b, preferred_element_type=F32)


def _sigmoid(x):
    return 0.5 * jnp.tanh(0.5 * x) + 0.5


def _swish(x):
    h = 0.5 * x
    return h * jnp.tanh(h) + h


def _resident(shape):
    zeros = (0,) * len(shape)
    return pl.BlockSpec(shape, lambda *_: zeros, pipeline_mode=pl.Buffered(1))


def _layer_resident(shape, layer, *block):
    idx = (layer,) + (block if block else (0,) * len(shape))
    return pl.BlockSpec((None,) + tuple(shape), lambda *_: idx, pipeline_mode=pl.Buffered(1))


def _params(sem):
    return pltpu.CompilerParams(dimension_semantics=sem, vmem_limit_bytes=VMEM_LIMIT_BYTES)


def _mod_kernel(cond_ref, w_ref, b_ref, dec_ref, mod_ref, lg_ref):
    c = cond_ref[...]
    s = (c * jax.nn.sigmoid(c)).astype(BF16)
    mod_ref[...] = _dot(s, w_ref[...].astype(BF16)) + b_ref[...]
    lg_ref[...] = jax.nn.log_sigmoid(dec_ref[...])


def _modulation(cond, w_mod, b_mod, decay):
    depth = w_mod.shape[0]
    nj = w_mod.shape[2] // D_MODEL
    return pl.pallas_call(
        _mod_kernel,
        grid=(depth, nj),
        in_specs=[
            pl.BlockSpec((MOD_ROWS, D_MODEL), lambda l, j: (0, 0)),
            pl.BlockSpec((None, D_MODEL, D_MODEL), lambda l, j: (l, 0, j)),
            pl.BlockSpec((None, 1, D_MODEL), lambda l, j: (l, 0, j)),
            pl.BlockSpec(decay.shape, lambda l, j: (0, 0)),
        ],
        out_specs=[
            pl.BlockSpec((None, MOD_ROWS, D_MODEL), lambda l, j: (l, 0, j)),
            pl.BlockSpec(decay.shape, lambda l, j: (0, 0)),
        ],
        out_shape=[
            jax.ShapeDtypeStruct((depth, MOD_ROWS, nj * D_MODEL), F32),
            jax.ShapeDtypeStruct(decay.shape, F32),
        ],
        compiler_params=_params(("arbitrary", "arbitrary")),
        name="modulation",
    )(cond, w_mod, b_mod.reshape(depth, 1, -1), decay)


def _inproj_kernel(*refs, rope, seg):
    if rope:
        (x_ref, mod_ref, g_ref, cos_ref, sin_ref, cost_ref, sint_ref,
         wu_ref, wq_ref, wk_ref, wv_ref, wrg_ref, wbg_ref, cw_ref, cb_ref, cg_ref,
         an_ref, q_ref, kt_ref, v_ref, rgs_ref, gab_ref, wkt_ref, h_scr, pad_scr, conv_scr) = refs
    else:
        (x_ref, mod_ref, g_ref,
         wu_ref, wq_ref, wk_ref, wv_ref, wrg_ref, wbg_ref, cw_ref, cb_ref, cg_ref,
         an_ref, q_ref, kt_ref, v_ref, rgs_ref, gab_ref, wkt_ref, h_scr, pad_scr, conv_scr) = refs
    bt = x_ref.shape[0]
    nseg = bt // seg
    pseg = seg + 2 * CONV_HALO
    nlc = CONV_CH // LANES

    @pl.when(pl.program_id(0) == 0)
    def _():
        wkt_ref[...] = wk_ref[...].T
        zeros = jnp.zeros((nlc, CONV_HALO, LANES), F32)
        for s in range(nseg):
            pad_scr[:, s * pseg:s * pseg + CONV_HALO, :] = zeros
            pad_scr[:, s * pseg + CONV_HALO + seg:(s + 1) * pseg, :] = zeros

    x = x_ref[...]
    ms = jnp.mean(x * x, axis=-1, keepdims=True)
    y = x * lax.rsqrt(ms + EPS) * g_ref[...]
    h_scr[...] = (y * (1.0 + mod_ref[1:2, :]) + mod_ref[0:1, :]).astype(BF16)

    ntile = bt // CONV_TILE
    tiles_per_seg = seg // CONV_TILE
    glu_w = 2 * LANES

    def glu_piece(p):
        u1 = _dot(h_scr[...], wu_ref[:, p * glu_w:(p + 1) * glu_w])
        u2 = _dot(h_scr[...], wu_ref[:, CONV_CH + p * glu_w:CONV_CH + (p + 1) * glu_w])
        a = u1 * _sigmoid(u2)
        for j in range(glu_w // LANES):
            lc = p * (glu_w // LANES) + j
            for s in range(nseg):
                pad_scr[lc, s * pseg + CONV_HALO:s * pseg + CONV_HALO + seg, :] = (
                    a[s * seg:(s + 1) * seg, j * LANES:(j + 1) * LANES])

    def conv_chunk(lc):
        for t in range(ntile):
            base = (t // tiles_per_seg) * pseg + (t % tiles_per_seg) * CONV_TILE + (CONV_HALO - CONV_PAD)
            acc = jnp.zeros((CONV_TILE, LANES), F32)
            for tap in range(CONV_W):
                acc = acc + pad_scr[lc, base + tap:base + tap + CONV_TILE, :] * cw_ref[lc, tap:tap + 1, :]
            conv_scr[lc, t * CONV_TILE:(t + 1) * CONV_TILE, :] = acc + cb_ref[lc]

    def conv_finish(t):
        rows = slice(t * CONV_TILE, (t + 1) * CONV_TILE)
        ssq = None
        for lc in range(nlc):
            cv = conv_scr[lc, rows, :]
            ssq = cv * cv if ssq is None else ssq + cv * cv
        scale = lax.rsqrt(jnp.sum(ssq, axis=-1, keepdims=True) * (1.0 / CONV_CH) + EPS)
        for lc in range(nlc):
            cn = conv_scr[lc, rows, :] * scale * cg_ref[lc]
            an_ref[rows, lc * LANES:(lc + 1) * LANES] = _swish(cn).astype(BF16)

    hd = RET_DK // 2
    nck = kt_ref.shape[0]
    cw = D_MODEL

    def q_proj():
        for hh in range(RET_HEADS):
            qh = _dot(h_scr[...], wq_ref[:, hh * RET_DK:(hh + 1) * RET_DK])
            if rope:
                x1, x2 = qh[:, :hd], qh[:, hd:]
                cos, sin = cos_ref[...], sin_ref[...]
                q_ref[hh, :, :hd] = (x1 * cos - x2 * sin).astype(BF16)
                q_ref[hh, :, hd:] = (x2 * cos + x1 * sin).astype(BF16)
            else:
                q_ref[hh] = qh.astype(BF16)

    def k_proj():
        for hh in range(RET_HEADS):
            kh = lax.dot_general(wkt_ref[hh * RET_DK:(hh + 1) * RET_DK, :], h_scr[...],
                                 (((1,), (1,)), ((), ())), preferred_element_type=F32)
            kh = kh * (RET_DK ** -0.5)
            if rope:
                x1, x2 = kh[:hd, :], kh[hd:, :]
                cos, sin = cost_ref[...], sint_ref[...]
                kh = jnp.concatenate([x1 * cos - x2 * sin, x2 * cos + x1 * sin], axis=0)
            kh = kh.astype(BF16)
            for j in range(nck):
                kt_ref[j, hh * RET_DK:(hh + 1) * RET_DK, :] = kh[:, j * RET_CHUNK:(j + 1) * RET_CHUNK]

    def v_proj(c):
        sl = slice(c * cw, (c + 1) * cw)
        r = _dot(h_scr[...], wv_ref[:, sl]).astype(BF16)
        for j in range(cw // RET_DV):
            v_ref[c * (cw // RET_DV) + j] = r[:, j * RET_DV:(j + 1) * RET_DV]

    def rg_proj(c):
        sl = slice(c * cw, (c + 1) * cw)
        r = _dot(h_scr[...], wrg_ref[:, sl])
        rgs_ref[:, sl] = _swish(r).astype(BF16)

    def bg_proj(c):
        sl = slice(c * cw, (c + 1) * cw)
        gab_ref[:, sl] = _sigmoid(_dot(h_scr[...], wbg_ref[:, sl])).astype(BF16)

    assert CONV_CH // glu_w == 4 and nlc == 8
    glu_piece(0); q_proj(); conv_chunk(0); conv_chunk(1)
    glu_piece(1); k_proj(); conv_chunk(2); conv_chunk(3)
    glu_piece(2); v_proj(0); conv_chunk(4); conv_chunk(5)
    v_proj(1); rg_proj(0); rg_proj(1); bg_proj(0)
    glu_piece(3); conv_chunk(6); conv_chunk(7)
    bg_proj(1)
    for t in range(ntile):
        conv_finish(t)


def _inproj(x, mod, layer, g, w_in, conv, tables, seq_len, seg):
    n = x.shape[0]
    bt = TOKEN_BLOCK
    rope = tables is not None
    per_seq = seq_len // bt
    nseg = bt // seg
    nlc = CONV_CH // LANES
    tok = lambda width: pl.BlockSpec((bt, width), lambda i: (i, 0))
    if rope:
        mod_spec = pl.BlockSpec((None, None, 6, D_MODEL), lambda i: (layer, 1 + i // per_seq, 0, 0))
    else:
        mod_spec = pl.BlockSpec((None, None, 6, D_MODEL), lambda i: (layer, 0, 0, 0))
    in_specs = [tok(D_MODEL), mod_spec, _layer_resident((1, D_MODEL), layer)]
    args = [x, mod, g]
    if rope:
        cos, sin, cost, sint = tables
        in_specs += [
            pl.BlockSpec((bt, RET_DK // 2), lambda i: (i % per_seq, 0)),
            pl.BlockSpec((bt, RET_DK // 2), lambda i: (i % per_seq, 0)),
            pl.BlockSpec((RET_DK // 2, bt), lambda i: (0, i % per_seq)),
            pl.BlockSpec((RET_DK // 2, bt), lambda i: (0, i % per_seq)),
        ]
        args += [cos, sin, cost, sint]
    wide, narrow = (D_MODEL, 2 * D_MODEL), (D_MODEL, D_MODEL)
    assert 2 * CONV_CH == RET_V_W == 2 * D_MODEL and RET_QK_W == D_MODEL
    in_specs += [
        _layer_resident(wide, layer, 0, 0), _layer_resident(narrow, layer, 0, 2),
        _layer_resident(narrow, layer, 0, 3), _layer_resident(wide, layer, 0, 2),
        _layer_resident(wide, layer, 0, 3), _layer_resident(wide, layer, 0, 4),
    ]
    args += [w_in] * 6
    in_specs += [_layer_resident(t.shape[1:], layer) for t in conv]
    args += list(conv)
    nck = bt // RET_CHUNK
    heads = lambda width: pl.BlockSpec((RET_HEADS, bt, width), lambda i: (0, i, 0))
    out_specs = [
        tok(CONV_CH), heads(RET_DK),
        pl.BlockSpec((nck, RET_QK_W, RET_CHUNK), lambda i: (i, 0, 0)),
        heads(RET_DV), tok(RET_V_W), tok(2 * D_MODEL),
    ]
    out_shape = [
        jax.ShapeDtypeStruct((n, CONV_CH), BF16),
        jax.ShapeDtypeStruct((RET_HEADS, n, RET_DK), BF16),
        jax.ShapeDtypeStruct((n // RET_CHUNK, RET_QK_W, RET_CHUNK), BF16),
        jax.ShapeDtypeStruct((RET_HEADS, n, RET_DV), BF16),
        jax.ShapeDtypeStruct((n, RET_V_W), BF16),
        jax.ShapeDtypeStruct((n, 2 * D_MODEL), BF16),
    ]
    return pl.pallas_call(
        functools.partial(_inproj_kernel, rope=rope, seg=seg),
        grid=(n // bt,),
        in_specs=in_specs,
        out_specs=out_specs,
        out_shape=out_shape,
        scratch_shapes=[
            pltpu.VMEM((RET_QK_W, D_MODEL), BF16),
            pltpu.VMEM((bt, D_MODEL), BF16),
            pltpu.VMEM((nlc, nseg * (seg + 2 * CONV_HALO), LANES), F32),
            pltpu.VMEM((nlc, bt, LANES), F32),
        ],
        compiler_params=_params(("arbitrary",)),
        name="inproj_lat" if rope else "inproj_ctx",
    )(*args)


def _decay_tables(lgf, lgb, qd_scr, kd_scr, dm_scr):
    c = RET_CHUNK
    if qd_scr is not None:
        ri = lax.broadcasted_iota(jnp.int32, (c, RET_DK), 0).astype(F32)
        qd_scr[0] = jnp.exp((ri + 1.0) * lgf).astype(BF16)
        qd_scr[1] = jnp.exp((c - ri) * lgb).astype(BF16)
    ci = lax.broadcasted_iota(jnp.int32, (RET_DK, c), 1).astype(F32)
    kd_scr[0] = jnp.exp((c - 1.0 - ci) * lgf).astype(BF16)
    kd_scr[1] = jnp.exp(ci * lgb).astype(BF16)
    di = (lax.broadcasted_iota(jnp.int32, (c, c), 0) - lax.broadcasted_iota(jnp.int32, (c, c), 1)).astype(F32)
    dm_scr[...] = (jnp.where(di >= 0, jnp.exp(jnp.maximum(di, 0.0) * lgf), 0.0)
                   + jnp.where(di <= 0, jnp.exp(jnp.maximum(-di, 0.0) * lgb), 0.0))


def _head_norm(y):
    mu = jnp.mean(y, axis=-1, keepdims=True)
    yc = y - mu
    var = jnp.mean(yc * yc, axis=-1, keepdims=True)
    return (yc * lax.rsqrt(var + EPS)).astype(BF16)


def _ret_lat_kernel(lg_ref, q_ref, kt_ref, v_ref, s0_ref, yn_ref,
                    qd_scr, kd_scr, dm_scr, sd_scr, sf_scr, sb_scr, sfh_scr, sbh_scr, *, layer, nchunk):
    c = RET_CHUNK
    h = pl.program_id(0)

    @pl.when(pl.program_id(1) == 0)
    def _():
        lgf = lg_ref[layer * 2 * RET_HEADS + h]
        lgb = lg_ref[layer * 2 * RET_HEADS + RET_HEADS + h]
        _decay_tables(lgf, lgb, qd_scr, kd_scr, dm_scr)
        sd_scr[0] = jnp.exp(jnp.full((SUBLANES, RET_DV), c * lgf, F32))
        sd_scr[1] = jnp.exp(jnp.full((SUBLANES, RET_DV), c * lgb, F32))

    sdf = sd_scr[0, 0:1, :]
    sdb = sd_scr[1, 0:1, :]

    def rows(i):
        return slice(i * c, (i + 1) * c)

    sb_scr[...] = s0_ref[1]
    sbh_scr[nchunk - 1] = s0_ref[1].astype(BF16)
    for ck in range(nchunk - 1, 0, -1):
        new = sb_scr[...] * sdb + _dot(kt_ref[ck] * kd_scr[1], v_ref[rows(ck), :])
        sbh_scr[ck - 1] = new.astype(BF16)
        if ck > 1:
            sb_scr[...] = new

    sf_scr[...] = s0_ref[0]
    sfh_scr[0] = s0_ref[0].astype(BF16)
    for ck in range(nchunk):
        qc = q_ref[rows(ck), :]
        kc = kt_ref[ck]
        vc = v_ref[rows(ck), :]
        pm = (_dot(qc, kc) * dm_scr[...]).astype(BF16)
        y = _dot(pm, vc) + _dot(qc * qd_scr[0], sfh_scr[ck]) + _dot(qc * qd_scr[1], sbh_scr[ck])
        yn_ref[rows(ck), :] = _head_norm(y)
        if ck < nchunk - 1:
            new = sf_scr[...] * sdf + _dot(kc * kd_scr[0], vc)
            sfh_scr[ck + 1] = new.astype(BF16)
            if ck < nchunk - 2:
                sf_scr[...] = new


def _retention_lat(lg, q, kt, v, state_ret, layer, seq_len):
    n = q.shape[1]
    c = RET_CHUNK
    nchunk = seq_len // c
    return pl.pallas_call(
        functools.partial(_ret_lat_kernel, layer=layer, nchunk=nchunk),
        grid=(RET_HEADS, n // seq_len),
        in_specs=[
            pl.BlockSpec(memory_space=pltpu.SMEM),
            pl.BlockSpec((None, seq_len, RET_DK), lambda h, b: (h, b, 0)),
            pl.BlockSpec((nchunk, RET_DK, c), lambda h, b: (b, h, 0)),
            pl.BlockSpec((None, seq_len, RET_DV), lambda h, b: (h, b, 0)),
            pl.BlockSpec((None, None, 2, None, RET_DK, RET_DV), lambda h, b: (b, layer, 0, h, 0, 0)),
        ],
        out_specs=pl.BlockSpec((None, seq_len, RET_DV), lambda h, b: (h, b, 0)),
        out_shape=jax.ShapeDtypeStruct((RET_HEADS, n, RET_DV), BF16),
        scratch_shapes=[
            pltpu.VMEM((2, c, RET_DK), BF16),
            pltpu.VMEM((2, RET_DK, c), BF16),
            pltpu.VMEM((c, c), F32),
            pltpu.VMEM((2, SUBLANES, RET_DV), F32),
            pltpu.VMEM((RET_DK, RET_DV), F32),
            pltpu.VMEM((RET_DK, RET_DV), F32),
            pltpu.VMEM((nchunk, RET_DK, RET_DV), BF16),
            pltpu.VMEM((nchunk, RET_DK, RET_DV), BF16),
        ],
        compiler_params=_params(("arbitrary", "arbitrary")),
        name="retention_lat",
    )(lg, q, kt, v, state_ret)


def _ret_ctx_kernel(*refs, layer, aliased):
    if aliased:
        refs = refs[:4] + refs[5:]
    lg_ref, q_ref, kt_ref, v_ref, yn_ref, st_ref, kd_scr, dm_scr = refs

    @pl.when(pl.program_id(0) == 0)
    def _():
        for h in range(RET_HEADS):
            lgf = lg_ref[layer * 2 * RET_HEADS + h]
            lgb = lg_ref[layer * 2 * RET_HEADS + RET_HEADS + h]
            _decay_tables(lgf, lgb, None, kd_scr.at[h], dm_scr.at[h])

    for h in range(RET_HEADS):
        qc = q_ref[h]
        kc = kt_ref[0, h * RET_DK:(h + 1) * RET_DK, :]
        vc = v_ref[h]
        pm = (_dot(qc, kc) * dm_scr[h]).astype(BF16)
        yn_ref[h] = _head_norm(_dot(pm, vc))
        st_ref[0, h] = _dot(kc * kd_scr[h, 0], vc)
        st_ref[1, h] = _dot(kc * kd_scr[h, 1], vc)


def _retention_ctx(lg, q, kt, v, states, layer, depth, seq_len):
    c = RET_CHUNK
    assert seq_len == c
    n = q.shape[1]
    nb = n // seq_len
    aliased = states is not None
    in_specs = [
        pl.BlockSpec(memory_space=pltpu.SMEM),
        pl.BlockSpec((RET_HEADS, seq_len, RET_DK), lambda b: (0, b, 0)),
        pl.BlockSpec((1, RET_QK_W, c), lambda b: (b, 0, 0)),
        pl.BlockSpec((RET_HEADS, seq_len, RET_DV), lambda b: (0, b, 0)),
    ]
    args = [lg, q, kt, v]
    if aliased:
        in_specs.append(pl.BlockSpec(memory_space=pl.ANY))
        args.append(states)
    return pl.pallas_call(
        functools.partial(_ret_ctx_kernel, layer=layer, aliased=aliased),
        grid=(nb,),
        in_specs=in_specs,
        out_specs=[
            pl.BlockSpec((RET_HEADS, seq_len, RET_DV), lambda b: (0, b, 0)),
            pl.BlockSpec((None, None, 2, RET_HEADS, RET_DK, RET_DV), lambda b: (b, layer, 0, 0, 0, 0)),
        ],
        out_shape=[
            jax.ShapeDtypeStruct((RET_HEADS, n, RET_DV), BF16),
            jax.ShapeDtypeStruct((nb, depth, 2, RET_HEADS, RET_DK, RET_DV), F32),
        ],
        input_output_aliases={4: 1} if aliased else {},
        scratch_shapes=[
            pltpu.VMEM((RET_HEADS, 2, RET_DK, c), BF16),
            pltpu.VMEM((RET_HEADS, c, c), F32),
        ],
        compiler_params=_params(("arbitrary",)),
        name="retention_ctx",
    )(*args)


def _outmlp_kernel(x_ref, an_ref, yn_ref, rgs_ref, gab_ref, mod_ref, g2_ref, gf_ref,
                   wco_ref, wro_ref, wo_ref, w1_ref, w2_ref, o_ref, *, final_norm):
    a = _dot(an_ref[...], wco_ref[...])
    b = None
    for h in range(RET_HEADS):
        hs = slice(h * RET_DV, (h + 1) * RET_DV)
        part = _dot(rgs_ref[:, hs] * yn_ref[h], wro_ref[hs, :])
        b = part if b is None else b + part
    m = (gab_ref[:, :D_MODEL].astype(F32) * a + gab_ref[:, D_MODEL:].astype(F32) * b).astype(BF16)
    x = x_ref[...] + mod_ref[2:3, :] * _dot(m, wo_ref[...])
    ms = jnp.mean(x * x, axis=-1, keepdims=True)
    h2 = (x * lax.rsqrt(ms + EPS) * g2_ref[...] * (1.0 + mod_ref[4:5, :]) + mod_ref[3:4, :]).astype(BF16)
    cw = 1024
    acc = None
    for c in range(MLP_W // cw):
        t = jnp.maximum(_dot(h2, w1_ref[:, c * cw:(c + 1) * cw]), 0.0)
        part = _dot((t * t).astype(BF16), w2_ref[c * cw:(c + 1) * cw, :])
        acc = part if acc is None else acc + part
    x = x + mod_ref[5:6, :] * acc
    if final_norm:
        ms = jnp.mean(x * x, axis=-1, keepdims=True)
        x = x * lax.rsqrt(ms + EPS) * gf_ref[...]
    o_ref[...] = x


def _outmlp(x, an, yn, rgs, gab, mod, layer, g2, gf, w, seq_len, latent, final_norm):
    n = x.shape[0]
    bt = TOKEN_BLOCK
    per_seq = seq_len // bt
    tok = lambda width: pl.BlockSpec((bt, width), lambda i: (i, 0))
    if latent:
        mod_spec = pl.BlockSpec((None, None, 6, D_MODEL), lambda i: (layer, 1 + i // per_seq, 0, 0))
    else:
        mod_spec = pl.BlockSpec((None, None, 6, D_MODEL), lambda i: (layer, 0, 0, 0))
    return pl.pallas_call(
        functools.partial(_outmlp_kernel, final_norm=final_norm),
        grid=(n // bt,),
        in_specs=[tok(D_MODEL), tok(CONV_CH), pl.BlockSpec((RET_HEADS, bt, RET_DV), lambda i: (0, i, 0)),
                  tok(RET_V_W), tok(2 * D_MODEL), mod_spec,
                  _layer_resident((1, D_MODEL), layer), _resident((1, D_MODEL))]
                 + [_layer_resident(wi.shape[1:], layer) for wi in w],
        out_specs=tok(D_MODEL),
        out_shape=jax.ShapeDtypeStruct((n, D_MODEL), F32),
        compiler_params=_params(("arbitrary",)),
        name="outmlp_lat" if latent else "outmlp_ctx",
    )(x, an, yn, rgs, gab, mod, g2, gf, *w)


def _rope_tables(seq_len):
    t = np.arange(seq_len)
    row = (t // GRID_W).astype(np.float32)
    col = (t % GRID_W).astype(np.float32)
    nf = RET_DK // 4
    inv = (np.float32(ROPE_BASE) ** (-np.arange(nf, dtype=np.float32) / np.float32(nf))).astype(np.float32)
    ang = np.concatenate([row[:, None] * inv, col[:, None] * inv], axis=-1).astype(np.float32)
    cos = np.cos(ang.astype(np.float64)).astype(np.float32)
    sin = np.sin(ang.astype(np.float64)).astype(np.float32)
    return tuple(jnp.asarray(t) for t in (cos, sin, np.ascontiguousarray(cos.T), np.ascontiguousarray(sin.T)))


def kernel(x_prompt, x_sample, c, state_ret, c_ctx, norm1_g, norm2_g, w_mod, b_mod, w_in, conv_dw, conv_b,
           conv_norm_g, w_conv_out, ret_decay_logit, w_ret_out, w_out, w_mlp1, w_mlp2, final_norm_g):
    depth = w_in.shape[0]
    nb_ctx, seq_ctx, _ = x_prompt.shape
    nb_lat, seq_lat, _ = x_sample.shape
    assert 1 + nb_lat <= MOD_ROWS
    assert TOKEN_BLOCK % seq_ctx == 0 and (nb_ctx * seq_ctx) % TOKEN_BLOCK == 0
    assert seq_lat % TOKEN_BLOCK == 0 and TOKEN_BLOCK % GRID_W == 0

    cond = jnp.concatenate([c_ctx[None, :], c, jnp.zeros((MOD_ROWS - 1 - nb_lat, D_MODEL), F32)], axis=0)
    mod, lg = _modulation(cond, w_mod, b_mod, ret_decay_logit.reshape(1, -1))
    mod = mod.reshape(depth, MOD_ROWS, 6, D_MODEL)
    lg = lg.reshape(-1)
    tables = _rope_tables(seq_lat)

    xp = x_prompt.reshape(nb_ctx * seq_ctx, D_MODEL)
    xs = x_sample.reshape(nb_lat * seq_lat, D_MODEL)
    w_in_h = w_in.astype(BF16)
    w_tail = tuple(t.astype(BF16) for t in (w_conv_out, w_ret_out, w_out, w_mlp1, w_mlp2))
    g1 = norm1_g.reshape(depth, 1, D_MODEL)
    g2 = norm2_g.reshape(depth, 1, D_MODEL)
    gf = final_norm_g.reshape(1, D_MODEL)
    nlc = CONV_CH // LANES
    conv = (conv_dw.reshape(depth, CONV_W, nlc, LANES).transpose(0, 2, 1, 3),
            conv_b.reshape(depth, nlc, 1, LANES), conv_norm_g.reshape(depth, nlc, 1, LANES))

    new_state = None
    for l in range(depth):
        last = l == depth - 1
        an, q, kt, v, rgs, gab = _inproj(xp, mod, l, g1, w_in_h, conv, None, seq_ctx, seq_ctx)
        yn, new_state = _retention_ctx(lg, q, kt, v, new_state, l, depth, seq_ctx)
        xp = _outmlp(xp, an, yn, rgs, gab, mod, l, g2, gf, w_tail, seq_ctx, False, last)

        an, q, kt, v, rgs, gab = _inproj(xs, mod, l, g1, w_in_h, conv, tables, seq_lat, GRID_W)
        yn = _retention_lat(lg, q, kt, v, state_ret, l, seq_lat)
        xs = _outmlp(xs, an, yn, rgs, gab, mod, l, g2, gf, w_tail, seq_lat, True, last)

    return (xp.reshape(x_prompt.shape), xs.reshape(x_sample.shape), new_state)
```

```python
import functools

import jax
import jax.numpy as jnp
import numpy as np
from jax import lax
from jax.experimental import pallas as pl
from jax.experimental.pallas import tpu as pltpu

D_MODEL = 1024
GRID_W = 64
CONV_CH = D_MODEL
CONV_W = 31
CONV_PAD = CONV_W // 2
RET_HEADS = 4
RET_DK = 256
RET_DV = 512
RET_QK_W = RET_HEADS * RET_DK
RET_V_W = RET_HEADS * RET_DV
MLP_W = 4 * D_MODEL
ROPE_BASE = 10000.0
EPS = 1e-6

RET_CHUNK = 256
TOKEN_BLOCK = 512
CONV_TILE = 64
CONV_HALO = 16
LANES = 128
SUBLANES = 8
MOD_ROWS = 16
VMEM_LIMIT_BYTES = 58 * 1024 * 1024

BF16 = jnp.bfloat16
F32 = jnp.float32


def _dot(a, b):
    return jnp.dot(a, b, preferred_element_type=F32)


def _sigmoid(x):
    return 0.5 * jnp.tanh(0.5 * x) + 0.5


def _swish(x):
    h = 0.5 * x
    return h * jnp.tanh(h) + h


def _resident(shape):
    zeros = (0,) * len(shape)
    return pl.BlockSpec(shape, lambda *_: zeros, pipeline_mode=pl.Buffered(1))


def _layer_resident(shape, layer, *block):
    idx = (layer,) + (block if block else (0,) * len(shape))
    return pl.BlockSpec((None,) + tuple(shape), lambda *_: idx, pipeline_mode=pl.Buffered(1))


def _params(sem):
    return pltpu.CompilerParams(dimension_semantics=sem, vmem_limit_bytes=VMEM_LIMIT_BYTES)


def _mod_kernel(cond_ref, w_ref, b_ref, dec_ref, mod_ref, lg_ref):
    c = cond_ref[...]
    s = (c * jax.nn.sigmoid(c)).astype(BF16)
    mod_ref[...] = _dot(s, w_ref[...].astype(BF16)) + b_ref[...]
    lg_ref[...] = jax.nn.log_sigmoid(dec_ref[...])


def _modulation(cond, w_mod, b_mod, decay):
    depth = w_mod.shape[0]
    nj = w_mod.shape[2] // D_MODEL
    return pl.pallas_call(
        _mod_kernel,
        grid=(depth, nj),
        in_specs=[
            pl.BlockSpec((MOD_ROWS, D_MODEL), lambda l, j: (0, 0)),
            pl.BlockSpec((None, D_MODEL, D_MODEL), lambda l, j: (l, 0, j)),
            pl.BlockSpec((None, 1, D_MODEL), lambda l, j: (l, 0, j)),
            pl.BlockSpec(decay.shape, lambda l, j: (0, 0)),
        ],
        out_specs=[
            pl.BlockSpec((None, MOD_ROWS, D_MODEL), lambda l, j: (l, 0, j)),
            pl.BlockSpec(decay.shape, lambda l, j: (0, 0)),
        ],
        out_shape=[
            jax.ShapeDtypeStruct((depth, MOD_ROWS, nj * D_MODEL), F32),
            jax.ShapeDtypeStruct(decay.shape, F32),
        ],
        compiler_params=_params(("arbitrary", "arbitrary")),
        name="modulation",
    )(cond, w_mod, b_mod.reshape(depth, 1, -1), decay)


def _inproj_kernel(*refs, rope, seg):
    if rope:
        (x_ref, mod_ref, g_ref, cos_ref, sin_ref, cost_ref, sint_ref,
         wu_ref, wq_ref, wk_ref, wv_ref, wrg_ref, wbg_ref, cw_ref, cb_ref, cg_ref,
         an_ref, q_ref, kt_ref, v_ref, rgs_ref, gab_ref, wkt_ref, h_scr, pad_scr, conv_scr) = refs
    else:
        (x_ref, mod_ref, g_ref,
         wu_ref, wq_ref, wk_ref, wv_ref, wrg_ref, wbg_ref, cw_ref, cb_ref, cg_ref,
         an_ref, q_ref, kt_ref, v_ref, rgs_ref, gab_ref, wkt_ref, h_scr, pad_scr, conv_scr) = refs
    bt = x_ref.shape[0]
    nseg = bt // seg
    pseg = seg + 2 * CONV_HALO
    nlc = CONV_CH // LANES

    @pl.when(pl.program_id(0) == 0)
    def _():
        wkt_ref[...] = wk_ref[...].T
        zeros = jnp.zeros((nlc, CONV_HALO, LANES), F32)
        for s in range(nseg):
            pad_scr[:, s * pseg:s * pseg + CONV_HALO, :] = zeros
            pad_scr[:, s * pseg + CONV_HALO + seg:(s + 1) * pseg, :] = zeros

    x = x_ref[...]
    ms = jnp.mean(x * x, axis=-1, keepdims=True)
    y = x * lax.rsqrt(ms + EPS) * g_ref[...]
    h_scr[...] = (y * (1.0 + mod_ref[1:2, :]) + mod_ref[0:1, :]).astype(BF16)

    ntile = bt // CONV_TILE
    tiles_per_seg = seg // CONV_TILE
    def glu_piece(lc):
        u = _dot(h_scr[...], wu_ref[:, lc * 2 * LANES:(lc + 1) * 2 * LANES])
        a = u[:, :LANES] * _sigmoid(u[:, LANES:])
        for s in range(nseg):
            pad_scr[lc, s * pseg + CONV_HALO:s * pseg + CONV_HALO + seg, :] = a[s * seg:(s + 1) * seg, :]

    def conv_chunk(lc):
        for t in range(ntile):
            base = (t // tiles_per_seg) * pseg + (t % tiles_per_seg) * CONV_TILE + (CONV_HALO - CONV_PAD)
            acc = jnp.zeros((CONV_TILE, LANES), F32)
            for tap in range(CONV_W):
                acc = acc + pad_scr[lc, base + tap:base + tap + CONV_TILE, :] * cw_ref[lc, tap:tap + 1, :]
            conv_scr[lc, t * CONV_TILE:(t + 1) * CONV_TILE, :] = acc + cb_ref[lc]

    def conv_finish(t):
        rows = slice(t * CONV_TILE, (t + 1) * CONV_TILE)
        ssq = None
        for lc in range(nlc):
            cv = conv_scr[lc, rows, :]
            ssq = cv * cv if ssq is None else ssq + cv * cv
        scale = lax.rsqrt(jnp.sum(ssq, axis=-1, keepdims=True) * (1.0 / CONV_CH) + EPS)
        for lc in range(nlc):
            cn = conv_scr[lc, rows, :] * scale * cg_ref[lc]
            an_ref[rows, lc * LANES:(lc + 1) * LANES] = _swish(cn).astype(BF16)

    hd = RET_DK // 2
    nck = kt_ref.shape[0]
    cw = D_MODEL

    def q_proj():
        for hh in range(RET_HEADS):
            qh = _dot(h_scr[...], wq_ref[:, hh * RET_DK:(hh + 1) * RET_DK])
            if rope:
                x1, x2 = qh[:, :hd], qh[:, hd:]
                cos, sin = cos_ref[...], sin_ref[...]
                q_ref[hh, :, :hd] = (x1 * cos - x2 * sin).astype(BF16)
                q_ref[hh, :, hd:] = (x2 * cos + x1 * sin).astype(BF16)
            else:
                q_ref[hh] = qh.astype(BF16)

    def k_proj():
        for hh in range(RET_HEADS):
            kh = lax.dot_general(wkt_ref[hh * RET_DK:(hh + 1) * RET_DK, :], h_scr[...],
                                 (((1,), (1,)), ((), ())), preferred_element_type=F32)
            kh = kh * (RET_DK ** -0.5)
            if rope:
                x1, x2 = kh[:hd, :], kh[hd:, :]
                cos, sin = cost_ref[...], sint_ref[...]
                kh = jnp.concatenate([x1 * cos - x2 * sin, x2 * cos + x1 * sin], axis=0)
            kh = kh.astype(BF16)
            for j in range(nck):
                kt_ref[j, hh * RET_DK:(hh + 1) * RET_DK, :] = kh[:, j * RET_CHUNK:(j + 1) * RET_CHUNK]

    def v_proj(c):
        sl = slice(c * cw, (c + 1) * cw)
        r = _dot(h_scr[...], wv_ref[:, sl]).astype(BF16)
        for j in range(cw // RET_DV):
            v_ref[c * (cw // RET_DV) + j] = r[:, j * RET_DV:(j + 1) * RET_DV]

    def rg_proj(c):
        sl = slice(c * cw, (c + 1) * cw)
        r = _dot(h_scr[...], wrg_ref[:, sl])
        rgs_ref[:, sl] = _swish(r).astype(BF16)

    def bg_proj(c):
        sl = slice(c * cw, (c + 1) * cw)
        gab_ref[:, sl] = _sigmoid(_dot(h_scr[...], wbg_ref[:, sl])).astype(BF16)

    assert nlc == 8
    glu_piece(0); glu_piece(1); q_proj(); conv_chunk(0); conv_chunk(1)
    glu_piece(2); glu_piece(3); k_proj(); conv_chunk(2); conv_chunk(3)
    glu_piece(4); glu_piece(5); v_proj(0); conv_chunk(4); conv_chunk(5)
    v_proj(1); rg_proj(0); glu_piece(6); conv_chunk(6); rg_proj(1); bg_proj(0)
    glu_piece(7); conv_chunk(7)
    bg_proj(1)
    for t in range(ntile):
        conv_finish(t)


def _inproj(x, mod, layer, g, w_in, w_glu, conv, tables, seq_len, seg):
    n = x.shape[0]
    bt = TOKEN_BLOCK
    rope = tables is not None
    per_seq = seq_len // bt
    nseg = bt // seg
    nlc = CONV_CH // LANES
    tok = lambda width: pl.BlockSpec((bt, width), lambda i: (i, 0))
    if rope:
        mod_spec = pl.BlockSpec((None, None, 6, D_MODEL), lambda i: (layer, 1 + i // per_seq, 0, 0))
    else:
        mod_spec = pl.BlockSpec((None, None, 6, D_MODEL), lambda i: (layer, 0, 0, 0))
    in_specs = [tok(D_MODEL), mod_spec, _layer_resident((1, D_MODEL), layer)]
    args = [x, mod, g]
    if rope:
        cos, sin, cost, sint = tables
        in_specs += [
            pl.BlockSpec((bt, RET_DK // 2), lambda i: (i % per_seq, 0)),
            pl.BlockSpec((bt, RET_DK // 2), lambda i: (i % per_seq, 0)),
            pl.BlockSpec((RET_DK // 2, bt), lambda i: (0, i % per_seq)),
            pl.BlockSpec((RET_DK // 2, bt), lambda i: (0, i % per_seq)),
        ]
        args += [cos, sin, cost, sint]
    wide, narrow = (D_MODEL, 2 * D_MODEL), (D_MODEL, D_MODEL)
    assert 2 * CONV_CH == RET_V_W == 2 * D_MODEL and RET_QK_W == D_MODEL
    in_specs += [
        _layer_resident(wide, layer), _layer_resident(narrow, layer, 0, 2),
        _layer_resident(narrow, layer, 0, 3), _layer_resident(wide, layer, 0, 2),
        _layer_resident(wide, layer, 0, 3), _layer_resident(wide, layer, 0, 4),
    ]
    args += [w_glu] + [w_in] * 5
    in_specs += [_layer_resident(t.shape[1:], layer) for t in conv]
    args += list(conv)
    nck = bt // RET_CHUNK
    heads = lambda width: pl.BlockSpec((RET_HEADS, bt, width), lambda i: (0, i, 0))
    out_specs = [
        tok(CONV_CH), heads(RET_DK),
        pl.BlockSpec((nck, RET_QK_W, RET_CHUNK), lambda i: (i, 0, 0)),
        heads(RET_DV), tok(RET_V_W), tok(2 * D_MODEL),
    ]
    out_shape = [
        jax.ShapeDtypeStruct((n, CONV_CH), BF16),
        jax.ShapeDtypeStruct((RET_HEADS, n, RET_DK), BF16),
        jax.ShapeDtypeStruct((n // RET_CHUNK, RET_QK_W, RET_CHUNK), BF16),
        jax.ShapeDtypeStruct((RET_HEADS, n, RET_DV), BF16),
        jax.ShapeDtypeStruct((n, RET_V_W), BF16),
        jax.ShapeDtypeStruct((n, 2 * D_MODEL), BF16),
    ]
    return pl.pallas_call(
        functools.partial(_inproj_kernel, rope=rope, seg=seg),
        grid=(n // bt,),
        in_specs=in_specs,
        out_specs=out_specs,
        out_shape=out_shape,
        scratch_shapes=[
            pltpu.VMEM((RET_QK_W, D_MODEL), BF16),
            pltpu.VMEM((bt, D_MODEL), BF16),
            pltpu.VMEM((nlc, nseg * (seg + 2 * CONV_HALO), LANES), F32),
            pltpu.VMEM((nlc, bt, LANES), F32),
        ],
        compiler_params=_params(("arbitrary",)),
        name="inproj_lat" if rope else "inproj_ctx",
    )(*args)


def _decay_tables(lgf, lgb, qd_scr, kd_scr, dm_scr):
    c = RET_CHUNK
    if qd_scr is not None:
        ri = lax.broadcasted_iota(jnp.int32, (c, RET_DK), 0).astype(F32)
        qd_scr[0] = jnp.exp((ri + 1.0) * lgf).astype(BF16)
        qd_scr[1] = jnp.exp((c - ri) * lgb).astype(BF16)
    ci = lax.broadcasted_iota(jnp.int32, (RET_DK, c), 1).astype(F32)
    kd_scr[0] = jnp.exp((c - 1.0 - ci) * lgf).astype(BF16)
    kd_scr[1] = jnp.exp(ci * lgb).astype(BF16)
    di = (lax.broadcasted_iota(jnp.int32, (c, c), 0) - lax.broadcasted_iota(jnp.int32, (c, c), 1)).astype(F32)
    dm_scr[...] = (jnp.where(di >= 0, jnp.exp(jnp.maximum(di, 0.0) * lgf), 0.0)
                   + jnp.where(di <= 0, jnp.exp(jnp.maximum(-di, 0.0) * lgb), 0.0))


def _head_norm(y):
    mu = jnp.mean(y, axis=-1, keepdims=True)
    yc = y - mu
    var = jnp.mean(yc * yc, axis=-1, keepdims=True)
    return (yc * lax.rsqrt(var + EPS)).astype(BF16)


def _ret_lat_kernel(lg_ref, q_ref, kt_ref, v_ref, s0_ref, yn_ref,
                    qd_scr, kd_scr, dm_scr, sd_scr, sf_scr, sb_scr, sfh_scr, sbh_scr, *, layer, nchunk):
    c = RET_CHUNK
    h = pl.program_id(0)

    @pl.when(pl.program_id(1) == 0)
    def _():
        lgf = lg_ref[layer * 2 * RET_HEADS + h]
        lgb = lg_ref[layer * 2 * RET_HEADS + RET_HEADS + h]
        _decay_tables(lgf, lgb, qd_scr, kd_scr, dm_scr)
        sd_scr[0] = jnp.exp(jnp.full((SUBLANES, RET_DV), c * lgf, F32))
        sd_scr[1] = jnp.exp(jnp.full((SUBLANES, RET_DV), c * lgb, F32))

    sdf = sd_scr[0, 0:1, :]
    sdb = sd_scr[1, 0:1, :]

    def rows(i):
        return slice(i * c, (i + 1) * c)

    sb_scr[...] = s0_ref[1]
    sbh_scr[nchunk - 1] = s0_ref[1].astype(BF16)
    for ck in range(nchunk - 1, 0, -1):
        new = sb_scr[...] * sdb + _dot(kt_ref[ck] * kd_scr[1], v_ref[rows(ck), :])
        sbh_scr[ck - 1] = new.astype(BF16)
        if ck > 1:
            sb_scr[...] = new

    sf_scr[...] = s0_ref[0]
    sfh_scr[0] = s0_ref[0].astype(BF16)
    for ck in range(nchunk):
        qc = q_ref[rows(ck), :]
        kc = kt_ref[ck]
        vc = v_ref[rows(ck), :]
        pm = (_dot(qc, kc) * dm_scr[...]).astype(BF16)
        y = _dot(pm, vc) + _dot(qc * qd_scr[0], sfh_scr[ck]) + _dot(qc * qd_scr[1], sbh_scr[ck])
        yn_ref[rows(ck), :] = _head_norm(y)
        if ck < nchunk - 1:
            new = sf_scr[...] * sdf + _dot(kc * kd_scr[0], vc)
            sfh_scr[ck + 1] = new.astype(BF16)
            if ck < nchunk - 2:
                sf_scr[...] = new


def _retention_lat(lg, q, kt, v, state_ret, layer, seq_len):
    n = q.shape[1]
    c = RET_CHUNK
    nchunk = seq_len // c
    return pl.pallas_call(
        functools.partial(_ret_lat_kernel, layer=layer, nchunk=nchunk),
        grid=(RET_HEADS, n // seq_len),
        in_specs=[
            pl.BlockSpec(memory_space=pltpu.SMEM),
            pl.BlockSpec((None, seq_len, RET_DK), lambda h, b: (h, b, 0)),
            pl.BlockSpec((nchunk, RET_DK, c), lambda h, b: (b, h, 0)),
            pl.BlockSpec((None, seq_len, RET_DV), lambda h, b: (h, b, 0)),
            pl.BlockSpec((None, None, 2, None, RET_DK, RET_DV), lambda h, b: (b, layer, 0, h, 0, 0)),
        ],
        out_specs=pl.BlockSpec((None, seq_len, RET_DV), lambda h, b: (h, b, 0)),
        out_shape=jax.ShapeDtypeStruct((RET_HEADS, n, RET_DV), BF16),
        scratch_shapes=[
            pltpu.VMEM((2, c, RET_DK), BF16),
            pltpu.VMEM((2, RET_DK, c), BF16),
            pltpu.VMEM((c, c), F32),
            pltpu.VMEM((2, SUBLANES, RET_DV), F32),
            pltpu.VMEM((RET_DK, RET_DV), F32),
            pltpu.VMEM((RET_DK, RET_DV), F32),
            pltpu.VMEM((nchunk, RET_DK, RET_DV), BF16),
            pltpu.VMEM((nchunk, RET_DK, RET_DV), BF16),
        ],
        compiler_params=_params(("arbitrary", "arbitrary")),
        name="retention_lat",
    )(lg, q, kt, v, state_ret)


def _ret_ctx_kernel(*refs, layer, aliased):
    if aliased:
        refs = refs[:4] + refs[5:]
    lg_ref, q_ref, kt_ref, v_ref, yn_ref, st_ref, kd_scr, dm_scr = refs

    @pl.when(pl.program_id(0) == 0)
    def _():
        for h in range(RET_HEADS):
            lgf = lg_ref[layer * 2 * RET_HEADS + h]
            lgb = lg_ref[layer * 2 * RET_HEADS + RET_HEADS + h]
            _decay_tables(lgf, lgb, None, kd_scr.at[h], dm_scr.at[h])

    for h in range(RET_HEADS):
        qc = q_ref[h]
        kc = kt_ref[0, h * RET_DK:(h + 1) * RET_DK, :]
        vc = v_ref[h]
        pm = (_dot(qc, kc) * dm_scr[h]).astype(BF16)
        yn_ref[h] = _head_norm(_dot(pm, vc))
        st_ref[0, h] = _dot(kc * kd_scr[h, 0], vc)
        st_ref[1, h] = _dot(kc * kd_scr[h, 1], vc)


def _retention_ctx(lg, q, kt, v, states, layer, depth, seq_len):
    c = RET_CHUNK
    assert seq_len == c
    n = q.shape[1]
    nb = n // seq_len
    aliased = states is not None
    in_specs = [
        pl.BlockSpec(memory_space=pltpu.SMEM),
        pl.BlockSpec((RET_HEADS, seq_len, RET_DK), lambda b: (0, b, 0)),
        pl.BlockSpec((1, RET_QK_W, c), lambda b: (b, 0, 0)),
        pl.BlockSpec((RET_HEADS, seq_len, RET_DV), lambda b: (0, b, 0)),
    ]
    args = [lg, q, kt, v]
    if aliased:
        in_specs.append(pl.BlockSpec(memory_space=pl.ANY))
        args.append(states)
    return pl.pallas_call(
        functools.partial(_ret_ctx_kernel, layer=layer, aliased=aliased),
        grid=(nb,),
        in_specs=in_specs,
        out_specs=[
            pl.BlockSpec((RET_HEADS, seq_len, RET_DV), lambda b: (0, b, 0)),
            pl.BlockSpec((None, None, 2, RET_HEADS, RET_DK, RET_DV), lambda b: (b, layer, 0, 0, 0, 0)),
        ],
        out_shape=[
            jax.ShapeDtypeStruct((RET_HEADS, n, RET_DV), BF16),
            jax.ShapeDtypeStruct((nb, depth, 2, RET_HEADS, RET_DK, RET_DV), F32),
        ],
        input_output_aliases={4: 1} if aliased else {},
        scratch_shapes=[
            pltpu.VMEM((RET_HEADS, 2, RET_DK, c), BF16),
            pltpu.VMEM((RET_HEADS, c, c), F32),
        ],
        compiler_params=_params(("arbitrary",)),
        name="retention_ctx",
    )(*args)


def _outmlp_kernel(x_ref, an_ref, yn_ref, rgs_ref, gab_ref, mod_ref, g2_ref, gf_ref,
                   wco_ref, wro_ref, wo_ref, w1_ref, w2_ref, o_ref, *, final_norm):
    a = _dot(an_ref[...], wco_ref[...])
    b = None
    for h in range(RET_HEADS):
        hs = slice(h * RET_DV, (h + 1) * RET_DV)
        part = _dot(rgs_ref[:, hs] * yn_ref[h], wro_ref[hs, :])
        b = part if b is None else b + part
    m = (gab_ref[:, :D_MODEL].astype(F32) * a + gab_ref[:, D_MODEL:].astype(F32) * b).astype(BF16)
    x = x_ref[...] + mod_ref[2:3, :] * _dot(m, wo_ref[...])
    ms = jnp.mean(x * x, axis=-1, keepdims=True)
    h2 = (x * lax.rsqrt(ms + EPS) * g2_ref[...] * (1.0 + mod_ref[4:5, :]) + mod_ref[3:4, :]).astype(BF16)
    cw = 1024
    acc = None
    for c in range(MLP_W // cw):
        t = jnp.maximum(_dot(h2, w1_ref[:, c * cw:(c + 1) * cw]), 0.0)
        part = _dot((t * t).astype(BF16), w2_ref[c * cw:(c + 1) * cw, :])
        acc = part if acc is None else acc + part
    x = x + mod_ref[5:6, :] * acc
    if final_norm:
        ms = jnp.mean(x * x, axis=-1, keepdims=True)
        x = x * lax.rsqrt(ms + EPS) * gf_ref[...]
    o_ref[...] = x


def _outmlp(x, an, yn, rgs, gab, mod, layer, g2, gf, w, seq_len, latent, final_norm):
    n = x.shape[0]
    bt = TOKEN_BLOCK
    per_seq = seq_len // bt
    tok = lambda width: pl.BlockSpec((bt, width), lambda i: (i, 0))
    if latent:
        mod_spec = pl.BlockSpec((None, None, 6, D_MODEL), lambda i: (layer, 1 + i // per_seq, 0, 0))
    else:
        mod_spec = pl.BlockSpec((None, None, 6, D_MODEL), lambda i: (layer, 0, 0, 0))
    return pl.pallas_call(
        functools.partial(_outmlp_kernel, final_norm=final_norm),
        grid=(n // bt,),
        in_specs=[tok(D_MODEL), tok(CONV_CH), pl.BlockSpec((RET_HEADS, bt, RET_DV), lambda i: (0, i, 0)),
                  tok(RET_V_W), tok(2 * D_MODEL), mod_spec,
                  _layer_resident((1, D_MODEL), layer), _resident((1, D_MODEL))]
                 + [_layer_resident(wi.shape[1:], layer) for wi in w],
        out_specs=tok(D_MODEL),
        out_shape=jax.ShapeDtypeStruct((n, D_MODEL), F32),
        compiler_params=_params(("arbitrary",)),
        name="outmlp_lat" if latent else "outmlp_ctx",
    )(x, an, yn, rgs, gab, mod, g2, gf, *w)


def _rope_tables(seq_len):
    t = np.arange(seq_len)
    row = (t // GRID_W).astype(np.float32)
    col = (t % GRID_W).astype(np.float32)
    nf = RET_DK // 4
    inv = (np.float32(ROPE_BASE) ** (-np.arange(nf, dtype=np.float32) / np.float32(nf))).astype(np.float32)
    ang = np.concatenate([row[:, None] * inv, col[:, None] * inv], axis=-1).astype(np.float32)
    cos = np.cos(ang.astype(np.float64)).astype(np.float32)
    sin = np.sin(ang.astype(np.float64)).astype(np.float32)
    return tuple(jnp.asarray(t) for t in (cos, sin, np.ascontiguousarray(cos.T), np.ascontiguousarray(sin.T)))


def kernel(x_prompt, x_sample, c, state_ret, c_ctx, norm1_g, norm2_g, w_mod, b_mod, w_in, conv_dw, conv_b,
           conv_norm_g, w_conv_out, ret_decay_logit, w_ret_out, w_out, w_mlp1, w_mlp2, final_norm_g):
    depth = w_in.shape[0]
    nb_ctx, seq_ctx, _ = x_prompt.shape
    nb_lat, seq_lat, _ = x_sample.shape
    assert 1 + nb_lat <= MOD_ROWS
    assert TOKEN_BLOCK % seq_ctx == 0 and (nb_ctx * seq_ctx) % TOKEN_BLOCK == 0
    assert seq_lat % TOKEN_BLOCK == 0 and TOKEN_BLOCK % GRID_W == 0

    cond = jnp.concatenate([c_ctx[None, :], c, jnp.zeros((MOD_ROWS - 1 - nb_lat, D_MODEL), F32)], axis=0)
    mod, lg = _modulation(cond, w_mod, b_mod, ret_decay_logit.reshape(1, -1))
    mod = mod.reshape(depth, MOD_ROWS, 6, D_MODEL)
    lg = lg.reshape(-1)
    tables = _rope_tables(seq_lat)

    xp = x_prompt.reshape(nb_ctx * seq_ctx, D_MODEL)
    xs = x_sample.reshape(nb_lat * seq_lat, D_MODEL)
    w_in_h = w_in.astype(BF16)
    nlc = CONV_CH // LANES
    w_glu = (w_in_h[:, :, :2 * CONV_CH].reshape(depth, D_MODEL, 2, nlc, LANES)
             .transpose(0, 1, 3, 2, 4).reshape(depth, D_MODEL, 2 * CONV_CH))
    w_tail = tuple(t.astype(BF16) for t in (w_conv_out, w_ret_out, w_out, w_mlp1, w_mlp2))
    g1 = norm1_g.reshape(depth, 1, D_MODEL)
    g2 = norm2_g.reshape(depth, 1, D_MODEL)
    gf = final_norm_g.reshape(1, D_MODEL)
    conv = (conv_dw.reshape(depth, CONV_W, nlc, LANES).transpose(0, 2, 1, 3),
            conv_b.reshape(depth, nlc, 1, LANES), conv_norm_g.reshape(depth, nlc, 1, LANES))

    new_state = None
    for l in range(depth):
        last = l == depth - 1
        an, q, kt, v, rgs, gab = _inproj(xp, mod, l, g1, w_in_h, w_glu, conv, None, seq_ctx, seq_ctx)
        yn, new_state = _retention_ctx(lg, q, kt, v, new_state, l, depth, seq_ctx)
        xp = _outmlp(xp, an, yn, rgs, gab, mod, l, g2, gf, w_tail, seq_ctx, False, last)

        an, q, kt, v, rgs, gab = _inproj(xs, mod, l, g1, w_in_h, w_glu, conv, tables, seq_lat, GRID_W)
        yn = _retention_lat(lg, q, kt, v, state_ret, l, seq_lat)
        xs = _outmlp(xs, an, yn, rgs, gab, mod, l, g2, gf, w_tail, seq_lat, True, last)

    return (xp.reshape(x_prompt.shape), xs.reshape(x_sample.shape), new_state)
```

```python
import functools

import jax
import jax.numpy as jnp
import numpy as np
from jax import lax
from jax.experimental import pallas as pl
from jax.experimental.pallas import tpu as pltpu

D_MODEL = 1024
GRID_W = 64
CONV_CH = D_MODEL
CONV_W = 31
CONV_PAD = CONV_W // 2
RET_HEADS = 4
RET_DK = 256
RET_DV = 512
RET_QK_W = RET_HEADS * RET_DK
RET_V_W = RET_HEADS * RET_DV
MLP_W = 4 * D_MODEL
ROPE_BASE = 10000.0
EPS = 1e-6

RET_CHUNK = 256
TOKEN_BLOCK = 512
CONV_TILE = 64
CONV_HALO = 16
LANES = 128
SUBLANES = 8
MOD_ROWS = 16
VMEM_LIMIT_BYTES = 58 * 1024 * 1024

BF16 = jnp.bfloat16
F32 = jnp.float32


def _dot(a, b):
    return jnp.dot(a, b, preferred_element_type=F32)


def _sigmoid(x):
    return 0.5 * jnp.tanh(0.5 * x) + 0.5


def _swish(x):
    h = 0.5 * x
    return h * jnp.tanh(h) + h


def _resident(shape):
    zeros = (0,) * len(shape)
    return pl.BlockSpec(shape, lambda *_: zeros, pipeline_mode=pl.Buffered(1))


def _layer_resident(shape, layer, *block):
    idx = (layer,) + (block if block else (0,) * len(shape))
    return pl.BlockSpec((None,) + tuple(shape), lambda *_: idx, pipeline_mode=pl.Buffered(1))


def _params(sem):
    return pltpu.CompilerParams(dimension_semantics=sem, vmem_limit_bytes=VMEM_LIMIT_BYTES)


def _mod_kernel(cond_ref, w_ref, b_ref, dec_ref, mod_ref, lg_ref):
    c = cond_ref[...]
    s = (c * jax.nn.sigmoid(c)).astype(BF16)
    mod_ref[...] = _dot(s, w_ref[...].astype(BF16)) + b_ref[...]
    lg_ref[...] = jax.nn.log_sigmoid(dec_ref[...])


def _modulation(cond, w_mod, b_mod, decay):
    depth = w_mod.shape[0]
    nj = w_mod.shape[2] // D_MODEL
    return pl.pallas_call(
        _mod_kernel,
        grid=(depth, nj),
        in_specs=[
            pl.BlockSpec((MOD_ROWS, D_MODEL), lambda l, j: (0, 0)),
            pl.BlockSpec((None, D_MODEL, D_MODEL), lambda l, j: (l, 0, j)),
            pl.BlockSpec((None, 1, D_MODEL), lambda l, j: (l, 0, j)),
            pl.BlockSpec(decay.shape, lambda l, j: (0, 0)),
        ],
        out_specs=[
            pl.BlockSpec((None, MOD_ROWS, D_MODEL), lambda l, j: (l, 0, j)),
            pl.BlockSpec(decay.shape, lambda l, j: (0, 0)),
        ],
        out_shape=[
            jax.ShapeDtypeStruct((depth, MOD_ROWS, nj * D_MODEL), F32),
            jax.ShapeDtypeStruct(decay.shape, F32),
        ],
        compiler_params=_params(("arbitrary", "arbitrary")),
        name="modulation",
    )(cond, w_mod, b_mod.reshape(depth, 1, -1), decay)


def _inproj_kernel(*refs, rope, seg):
    if rope:
        (x_ref, mod_ref, g_ref, cos_ref, sin_ref, cost_ref, sint_ref,
         wu_ref, wq_ref, wk_ref, wv_ref, wrg_ref, wbg_ref, cw_ref, cb_ref, cg_ref,
         an_ref, q_ref, kt_ref, v_ref, rgs_ref, gab_ref, wkt_ref, h_scr, pad_scr, conv_scr, wg_scr) = refs
    else:
        (x_ref, mod_ref, g_ref,
         wu_ref, wq_ref, wk_ref, wv_ref, wrg_ref, wbg_ref, cw_ref, cb_ref, cg_ref,
         an_ref, q_ref, kt_ref, v_ref, rgs_ref, gab_ref, wkt_ref, h_scr, pad_scr, conv_scr, wg_scr) = refs
    bt = x_ref.shape[0]
    nseg = bt // seg
    pseg = seg + 2 * CONV_HALO
    nlc = CONV_CH // LANES

    @pl.when(pl.program_id(0) == 0)
    def _():
        wkt_ref[...] = wk_ref[...].T
        for c in range(nlc):
            wg_scr[:, (2 * c) * LANES:(2 * c + 1) * LANES] = wu_ref[:, c * LANES:(c + 1) * LANES]
            wg_scr[:, (2 * c + 1) * LANES:(2 * c + 2) * LANES] = (
                wu_ref[:, CONV_CH + c * LANES:CONV_CH + (c + 1) * LANES])
        zeros = jnp.zeros((nlc, CONV_HALO, LANES), F32)
        for s in range(nseg):
            pad_scr[:, s * pseg:s * pseg + CONV_HALO, :] = zeros
            pad_scr[:, s * pseg + CONV_HALO + seg:(s + 1) * pseg, :] = zeros

    x = x_ref[...]
    ms = jnp.mean(x * x, axis=-1, keepdims=True)
    y = x * lax.rsqrt(ms + EPS) * g_ref[...]
    h_scr[...] = (y * (1.0 + mod_ref[1:2, :]) + mod_ref[0:1, :]).astype(BF16)

    ntile = bt // CONV_TILE
    tiles_per_seg = seg // CONV_TILE
    def glu_piece(lc):
        u = _dot(h_scr[...], wg_scr[:, lc * 2 * LANES:(lc + 1) * 2 * LANES])
        a = u[:, :LANES] * _sigmoid(u[:, LANES:])
        for s in range(nseg):
            pad_scr[lc, s * pseg + CONV_HALO:s * pseg + CONV_HALO + seg, :] = a[s * seg:(s + 1) * seg, :]

    def conv_chunk(lc):
        for t in range(ntile):
            base = (t // tiles_per_seg) * pseg + (t % tiles_per_seg) * CONV_TILE + (CONV_HALO - CONV_PAD)
            acc = jnp.zeros((CONV_TILE, LANES), F32)
            for tap in range(CONV_W):
                acc = acc + pad_scr[lc, base + tap:base + tap + CONV_TILE, :] * cw_ref[lc, tap:tap + 1, :]
            conv_scr[lc, t * CONV_TILE:(t + 1) * CONV_TILE, :] = acc + cb_ref[lc]

    def conv_finish(t):
        rows = slice(t * CONV_TILE, (t + 1) * CONV_TILE)
        ssq = None
        for lc in range(nlc):
            cv = conv_scr[lc, rows, :]
            ssq = cv * cv if ssq is None else ssq + cv * cv
        scale = lax.rsqrt(jnp.sum(ssq, axis=-1, keepdims=True) * (1.0 / CONV_CH) + EPS)
        for lc in range(nlc):
            cn = conv_scr[lc, rows, :] * scale * cg_ref[lc]
            an_ref[rows, lc * LANES:(lc + 1) * LANES] = _swish(cn).astype(BF16)

    hd = RET_DK // 2
    nck = kt_ref.shape[0]
    cw = D_MODEL

    def q_proj():
        for hh in range(RET_HEADS):
            qh = _dot(h_scr[...], wq_ref[:, hh * RET_DK:(hh + 1) * RET_DK])
            if rope:
                x1, x2 = qh[:, :hd], qh[:, hd:]
                cos, sin = cos_ref[...], sin_ref[...]
                q_ref[hh, :, :hd] = (x1 * cos - x2 * sin).astype(BF16)
                q_ref[hh, :, hd:] = (x2 * cos + x1 * sin).astype(BF16)
            else:
                q_ref[hh] = qh.astype(BF16)

    def k_proj():
        for hh in range(RET_HEADS):
            kh = lax.dot_general(wkt_ref[hh * RET_DK:(hh + 1) * RET_DK, :], h_scr[...],
                                 (((1,), (1,)), ((), ())), preferred_element_type=F32)
            kh = kh * (RET_DK ** -0.5)
            if rope:
                x1, x2 = kh[:hd, :], kh[hd:, :]
                cos, sin = cost_ref[...], sint_ref[...]
                kh = jnp.concatenate([x1 * cos - x2 * sin, x2 * cos + x1 * sin], axis=0)
            kh = kh.astype(BF16)
            for j in range(nck):
                kt_ref[j, hh * RET_DK:(hh + 1) * RET_DK, :] = kh[:, j * RET_CHUNK:(j + 1) * RET_CHUNK]

    def v_proj(c):
        sl = slice(c * cw, (c + 1) * cw)
        r = _dot(h_scr[...], wv_ref[:, sl]).astype(BF16)
        for j in range(cw // RET_DV):
            v_ref[c * (cw // RET_DV) + j] = r[:, j * RET_DV:(j + 1) * RET_DV]

    def rg_proj(c):
        sl = slice(c * cw, (c + 1) * cw)
        r = _dot(h_scr[...], wrg_ref[:, sl])
        rgs_ref[:, sl] = _swish(r).astype(BF16)

    def bg_proj(c):
        sl = slice(c * cw, (c + 1) * cw)
        gab_ref[:, sl] = _sigmoid(_dot(h_scr[...], wbg_ref[:, sl])).astype(BF16)

    assert nlc == 8
    glu_piece(0); glu_piece(1); q_proj(); conv_chunk(0); conv_chunk(1)
    glu_piece(2); glu_piece(3); k_proj(); conv_chunk(2); conv_chunk(3)
    glu_piece(4); glu_piece(5); v_proj(0); conv_chunk(4); conv_chunk(5)
    v_proj(1); rg_proj(0); glu_piece(6); conv_chunk(6); rg_proj(1); bg_proj(0)
    glu_piece(7); conv_chunk(7)
    bg_proj(1)
    for t in range(ntile):
        conv_finish(t)


def _inproj(x, mod, layer, g, w_in, conv, tables, seq_len, seg):
    n = x.shape[0]
    bt = TOKEN_BLOCK
    rope = tables is not None
    per_seq = seq_len // bt
    nseg = bt // seg
    nlc = CONV_CH // LANES
    tok = lambda width: pl.BlockSpec((bt, width), lambda i: (i, 0))
    if rope:
        mod_spec = pl.BlockSpec((None, None, 6, D_MODEL), lambda i: (layer, 1 + i // per_seq, 0, 0))
    else:
        mod_spec = pl.BlockSpec((None, None, 6, D_MODEL), lambda i: (layer, 0, 0, 0))
    in_specs = [tok(D_MODEL), mod_spec, _layer_resident((1, D_MODEL), layer)]
    args = [x, mod, g]
    if rope:
        cos, sin, cost, sint = tables
        in_specs += [
            pl.BlockSpec((bt, RET_DK // 2), lambda i: (i % per_seq, 0)),
            pl.BlockSpec((bt, RET_DK // 2), lambda i: (i % per_seq, 0)),
            pl.BlockSpec((RET_DK // 2, bt), lambda i: (0, i % per_seq)),
            pl.BlockSpec((RET_DK // 2, bt), lambda i: (0, i % per_seq)),
        ]
        args += [cos, sin, cost, sint]
    wide, narrow = (D_MODEL, 2 * D_MODEL), (D_MODEL, D_MODEL)
    assert 2 * CONV_CH == RET_V_W == 2 * D_MODEL and RET_QK_W == D_MODEL
    in_specs += [
        _layer_resident(wide, layer, 0, 0), _layer_resident(narrow, layer, 0, 2),
        _layer_resident(narrow, layer, 0, 3), _layer_resident(wide, layer, 0, 2),
        _layer_resident(wide, layer, 0, 3), _layer_resident(wide, layer, 0, 4),
    ]
    args += [w_in] * 6
    in_specs += [_layer_resident(t.shape[1:], layer) for t in conv]
    args += list(conv)
    nck = bt // RET_CHUNK
    heads = lambda width: pl.BlockSpec((RET_HEADS, bt, width), lambda i: (0, i, 0))
    out_specs = [
        tok(CONV_CH), heads(RET_DK),
        pl.BlockSpec((nck, RET_QK_W, RET_CHUNK), lambda i: (i, 0, 0)),
        heads(RET_DV), tok(RET_V_W), tok(2 * D_MODEL),
    ]
    out_shape = [
        jax.ShapeDtypeStruct((n, CONV_CH), BF16),
        jax.ShapeDtypeStruct((RET_HEADS, n, RET_DK), BF16),
        jax.ShapeDtypeStruct((n // RET_CHUNK, RET_QK_W, RET_CHUNK), BF16),
        jax.ShapeDtypeStruct((RET_HEADS, n, RET_DV), BF16),
        jax.ShapeDtypeStruct((n, RET_V_W), BF16),
        jax.ShapeDtypeStruct((n, 2 * D_MODEL), BF16),
    ]
    return pl.pallas_call(
        functools.partial(_inproj_kernel, rope=rope, seg=seg),
        grid=(n // bt,),
        in_specs=in_specs,
        out_specs=out_specs,
        out_shape=out_shape,
        scratch_shapes=[
            pltpu.VMEM((RET_QK_W, D_MODEL), BF16),
            pltpu.VMEM((bt, D_MODEL), BF16),
            pltpu.VMEM((nlc, nseg * (seg + 2 * CONV_HALO), LANES), F32),
            pltpu.VMEM((nlc, bt, LANES), F32),
            pltpu.VMEM((D_MODEL, 2 * CONV_CH), BF16),
        ],
        compiler_params=_params(("arbitrary",)),
        name="inproj_lat" if rope else "inproj_ctx",
    )(*args)


def _decay_tables(lgf, lgb, qd_scr, kd_scr, dm_scr):
    c = RET_CHUNK
    if qd_scr is not None:
        ri = lax.broadcasted_iota(jnp.int32, (c, RET_DK), 0).astype(F32)
        qd_scr[0] = jnp.exp((ri + 1.0) * lgf).astype(BF16)
        qd_scr[1] = jnp.exp((c - ri) * lgb).astype(BF16)
    ci = lax.broadcasted_iota(jnp.int32, (RET_DK, c), 1).astype(F32)
    kd_scr[0] = jnp.exp((c - 1.0 - ci) * lgf).astype(BF16)
    kd_scr[1] = jnp.exp(ci * lgb).astype(BF16)
    di = (lax.broadcasted_iota(jnp.int32, (c, c), 0) - lax.broadcasted_iota(jnp.int32, (c, c), 1)).astype(F32)
    dm_scr[...] = (jnp.where(di >= 0, jnp.exp(jnp.maximum(di, 0.0) * lgf), 0.0)
                   + jnp.where(di <= 0, jnp.exp(jnp.maximum(-di, 0.0) * lgb), 0.0))


def _head_norm(y):
    mu = jnp.mean(y, axis=-1, keepdims=True)
    yc = y - mu
    var = jnp.mean(yc * yc, axis=-1, keepdims=True)
    return (yc * lax.rsqrt(var + EPS)).astype(BF16)


def _ret_lat_kernel(lg_ref, q_ref, kt_ref, v_ref, s0_ref, yn_ref,
                    qd_scr, kd_scr, dm_scr, sd_scr, sf_scr, sb_scr, sfh_scr, sbh_scr, *, layer, nchunk):
    c = RET_CHUNK
    h = pl.program_id(0)

    @pl.when(pl.program_id(1) == 0)
    def _():
        lgf = lg_ref[layer * 2 * RET_HEADS + h]
        lgb = lg_ref[layer * 2 * RET_HEADS + RET_HEADS + h]
        _decay_tables(lgf, lgb, qd_scr, kd_scr, dm_scr)
        sd_scr[0] = jnp.exp(jnp.full((SUBLANES, RET_DV), c * lgf, F32))
        sd_scr[1] = jnp.exp(jnp.full((SUBLANES, RET_DV), c * lgb, F32))

    sdf = sd_scr[0, 0:1, :]
    sdb = sd_scr[1, 0:1, :]

    def rows(i):
        return slice(i * c, (i + 1) * c)

    sb_scr[...] = s0_ref[1]
    sbh_scr[nchunk - 1] = s0_ref[1].astype(BF16)
    for ck in range(nchunk - 1, 0, -1):
        new = sb_scr[...] * sdb + _dot(kt_ref[ck] * kd_scr[1], v_ref[rows(ck), :])
        sbh_scr[ck - 1] = new.astype(BF16)
        if ck > 1:
            sb_scr[...] = new

    sf_scr[...] = s0_ref[0]
    sfh_scr[0] = s0_ref[0].astype(BF16)
    for ck in range(nchunk):
        qc = q_ref[rows(ck), :]
        kc = kt_ref[ck]
        vc = v_ref[rows(ck), :]
        pm = (_dot(qc, kc) * dm_scr[...]).astype(BF16)
        y = _dot(pm, vc) + _dot(qc * qd_scr[0], sfh_scr[ck]) + _dot(qc * qd_scr[1], sbh_scr[ck])
        yn_ref[rows(ck), :] = _head_norm(y)
        if ck < nchunk - 1:
            new = sf_scr[...] * sdf + _dot(kc * kd_scr[0], vc)
            sfh_scr[ck + 1] = new.astype(BF16)
            if ck < nchunk - 2:
                sf_scr[...] = new


def _retention_lat(lg, q, kt, v, state_ret, layer, seq_len):
    n = q.shape[1]
    c = RET_CHUNK
    nchunk = seq_len // c
    return pl.pallas_call(
        functools.partial(_ret_lat_kernel, layer=layer, nchunk=nchunk),
        grid=(RET_HEADS, n // seq_len),
        in_specs=[
            pl.BlockSpec(memory_space=pltpu.SMEM),
            pl.BlockSpec((None, seq_len, RET_DK), lambda h, b: (h, b, 0)),
            pl.BlockSpec((nchunk, RET_DK, c), lambda h, b: (b, h, 0)),
            pl.BlockSpec((None, seq_len, RET_DV), lambda h, b: (h, b, 0)),
            pl.BlockSpec((None, None, 2, None, RET_DK, RET_DV), lambda h, b: (b, layer, 0, h, 0, 0)),
        ],
        out_specs=pl.BlockSpec((None, seq_len, RET_DV), lambda h, b: (h, b, 0)),
        out_shape=jax.ShapeDtypeStruct((RET_HEADS, n, RET_DV), BF16),
        scratch_shapes=[
            pltpu.VMEM((2, c, RET_DK), BF16),
            pltpu.VMEM((2, RET_DK, c), BF16),
            pltpu.VMEM((c, c), F32),
            pltpu.VMEM((2, SUBLANES, RET_DV), F32),
            pltpu.VMEM((RET_DK, RET_DV), F32),
            pltpu.VMEM((RET_DK, RET_DV), F32),
            pltpu.VMEM((nchunk, RET_DK, RET_DV), BF16),
            pltpu.VMEM((nchunk, RET_DK, RET_DV), BF16),
        ],
        compiler_params=_params(("arbitrary", "arbitrary")),
        name="retention_lat",
    )(lg, q, kt, v, state_ret)


def _ret_ctx_kernel(*refs, layer, aliased):
    if aliased:
        refs = refs[:4] + refs[5:]
    lg_ref, q_ref, kt_ref, v_ref, yn_ref, st_ref, kd_scr, dm_scr = refs

    @pl.when(pl.program_id(0) == 0)
    def _():
        for h in range(RET_HEADS):
            lgf = lg_ref[layer * 2 * RET_HEADS + h]
            lgb = lg_ref[layer * 2 * RET_HEADS + RET_HEADS + h]
            _decay_tables(lgf, lgb, None, kd_scr.at[h], dm_scr.at[h])

    for h in range(RET_HEADS):
        qc = q_ref[h]
        kc = kt_ref[0, h * RET_DK:(h + 1) * RET_DK, :]
        vc = v_ref[h]
        pm = (_dot(qc, kc) * dm_scr[h]).astype(BF16)
        yn_ref[h] = _head_norm(_dot(pm, vc))
        st_ref[0, h] = _dot(kc * kd_scr[h, 0], vc)
        st_ref[1, h] = _dot(kc * kd_scr[h, 1], vc)


def _retention_ctx(lg, q, kt, v, states, layer, depth, seq_len):
    c = RET_CHUNK
    assert seq_len == c
    n = q.shape[1]
    nb = n // seq_len
    aliased = states is not None
    in_specs = [
        pl.BlockSpec(memory_space=pltpu.SMEM),
        pl.BlockSpec((RET_HEADS, seq_len, RET_DK), lambda b: (0, b, 0)),
        pl.BlockSpec((1, RET_QK_W, c), lambda b: (b, 0, 0)),
        pl.BlockSpec((RET_HEADS, seq_len, RET_DV), lambda b: (0, b, 0)),
    ]
    args = [lg, q, kt, v]
    if aliased:
        in_specs.append(pl.BlockSpec(memory_space=pl.ANY))
        args.append(states)
    return pl.pallas_call(
        functools.partial(_ret_ctx_kernel, layer=layer, aliased=aliased),
        grid=(nb,),
        in_specs=in_specs,
        out_specs=[
            pl.BlockSpec((RET_HEADS, seq_len, RET_DV), lambda b: (0, b, 0)),
            pl.BlockSpec((None, None, 2, RET_HEADS, RET_DK, RET_DV), lambda b: (b, layer, 0, 0, 0, 0)),
        ],
        out_shape=[
            jax.ShapeDtypeStruct((RET_HEADS, n, RET_DV), BF16),
            jax.ShapeDtypeStruct((nb, depth, 2, RET_HEADS, RET_DK, RET_DV), F32),
        ],
        input_output_aliases={4: 1} if aliased else {},
        scratch_shapes=[
            pltpu.VMEM((RET_HEADS, 2, RET_DK, c), BF16),
            pltpu.VMEM((RET_HEADS, c, c), F32),
        ],
        compiler_params=_params(("arbitrary",)),
        name="retention_ctx",
    )(*args)


def _outmlp_kernel(x_ref, an_ref, yn_ref, rgs_ref, gab_ref, mod_ref, g2_ref, gf_ref,
                   wco_ref, wro_ref, wo_ref, w1_ref, w2_ref, o_ref, *, final_norm):
    a = _dot(an_ref[...], wco_ref[...])
    b = None
    for h in range(RET_HEADS):
        hs = slice(h * RET_DV, (h + 1) * RET_DV)
        part = _dot(rgs_ref[:, hs] * yn_ref[h], wro_ref[hs, :])
        b = part if b is None else b + part
    m = (gab_ref[:, :D_MODEL].astype(F32) * a + gab_ref[:, D_MODEL:].astype(F32) * b).astype(BF16)
    x = x_ref[...] + mod_ref[2:3, :] * _dot(m, wo_ref[...])
    ms = jnp.mean(x * x, axis=-1, keepdims=True)
    h2 = (x * lax.rsqrt(ms + EPS) * g2_ref[...] * (1.0 + mod_ref[4:5, :]) + mod_ref[3:4, :]).astype(BF16)
    cw = 1024
    acc = None
    for c in range(MLP_W // cw):
        t = jnp.maximum(_dot(h2, w1_ref[:, c * cw:(c + 1) * cw]), 0.0)
        part = _dot((t * t).astype(BF16), w2_ref[c * cw:(c + 1) * cw, :])
        acc = part if acc is None else acc + part
    x = x + mod_ref[5:6, :] * acc
    if final_norm:
        ms = jnp.mean(x * x, axis=-1, keepdims=True)
        x = x * lax.rsqrt(ms + EPS) * gf_ref[...]
    o_ref[...] = x


def _outmlp(x, an, yn, rgs, gab, mod, layer, g2, gf, w, seq_len, latent, final_norm):
    n = x.shape[0]
    bt = TOKEN_BLOCK
    per_seq = seq_len // bt
    tok = lambda width: pl.BlockSpec((bt, width), lambda i: (i, 0))
    if latent:
        mod_spec = pl.BlockSpec((None, None, 6, D_MODEL), lambda i: (layer, 1 + i // per_seq, 0, 0))
    else:
        mod_spec = pl.BlockSpec((None, None, 6, D_MODEL), lambda i: (layer, 0, 0, 0))
    return pl.pallas_call(
        functools.partial(_outmlp_kernel, final_norm=final_norm),
        grid=(n // bt,),
        in_specs=[tok(D_MODEL), tok(CONV_CH), pl.BlockSpec((RET_HEADS, bt, RET_DV), lambda i: (0, i, 0)),
                  tok(RET_V_W), tok(2 * D_MODEL), mod_spec,
                  _layer_resident((1, D_MODEL), layer), _resident((1, D_MODEL))]
                 + [_layer_resident(wi.shape[1:], layer) for wi in w],
        out_specs=tok(D_MODEL),
        out_shape=jax.ShapeDtypeStruct((n, D_MODEL), F32),
        compiler_params=_params(("arbitrary",)),
        name="outmlp_lat" if latent else "outmlp_ctx",
    )(x, an, yn, rgs, gab, mod, g2, gf, *w)


def _rope_tables(seq_len):
    t = np.arange(seq_len)
    row = (t // GRID_W).astype(np.float32)
    col = (t % GRID_W).astype(np.float32)
    nf = RET_DK // 4
    inv = (np.float32(ROPE_BASE) ** (-np.arange(nf, dtype=np.float32) / np.float32(nf))).astype(np.float32)
    ang = np.concatenate([row[:, None] * inv, col[:, None] * inv], axis=-1).astype(np.float32)
    cos = np.cos(ang.astype(np.float64)).astype(np.float32)
    sin = np.sin(ang.astype(np.float64)).astype(np.float32)
    return tuple(jnp.asarray(t) for t in (cos, sin, np.ascontiguousarray(cos.T), np.ascontiguousarray(sin.T)))


def kernel(x_prompt, x_sample, c, state_ret, c_ctx, norm1_g, norm2_g, w_mod, b_mod, w_in, conv_dw, conv_b,
           conv_norm_g, w_conv_out, ret_decay_logit, w_ret_out, w_out, w_mlp1, w_mlp2, final_norm_g):
    depth = w_in.shape[0]
    nb_ctx, seq_ctx, _ = x_prompt.shape
    nb_lat, seq_lat, _ = x_sample.shape
    assert 1 + nb_lat <= MOD_ROWS
    assert TOKEN_BLOCK % seq_ctx == 0 and (nb_ctx * seq_ctx) % TOKEN_BLOCK == 0
    assert seq_lat % TOKEN_BLOCK == 0 and TOKEN_BLOCK % GRID_W == 0

    cond = jnp.concatenate([c_ctx[None, :], c, jnp.zeros((MOD_ROWS - 1 - nb_lat, D_MODEL), F32)], axis=0)
    mod, lg = _modulation(cond, w_mod, b_mod, ret_decay_logit.reshape(1, -1))
    mod = mod.reshape(depth, MOD_ROWS, 6, D_MODEL)
    lg = lg.reshape(-1)
    tables = _rope_tables(seq_lat)

    xp = x_prompt.reshape(nb_ctx * seq_ctx, D_MODEL)
    xs = x_sample.reshape(nb_lat * seq_lat, D_MODEL)
    w_in_h = w_in.astype(BF16)
    nlc = CONV_CH // LANES
    w_tail = tuple(t.astype(BF16) for t in (w_conv_out, w_ret_out, w_out, w_mlp1, w_mlp2))
    g1 = norm1_g.reshape(depth, 1, D_MODEL)
    g2 = norm2_g.reshape(depth, 1, D_MODEL)
    gf = final_norm_g.reshape(1, D_MODEL)
    conv = (conv_dw.reshape(depth, CONV_W, nlc, LANES).transpose(0, 2, 1, 3),
            conv_b.reshape(depth, nlc, 1, LANES), conv_norm_g.reshape(depth, nlc, 1, LANES))

    new_state = None
    for l in range(depth):
        last = l == depth - 1
        an, q, kt, v, rgs, gab = _inproj(xp, mod, l, g1, w_in_h, conv, None, seq_ctx, seq_ctx)
        yn, new_state = _retention_ctx(lg, q, kt, v, new_state, l, depth, seq_ctx)
        xp = _outmlp(xp, an, yn, rgs, gab, mod, l, g2, gf, w_tail, seq_ctx, False, last)

        an, q, kt, v, rgs, gab = _inproj(xs, mod, l, g1, w_in_h, conv, tables, seq_lat, GRID_W)
        yn = _retention_lat(lg, q, kt, v, state_ret, l, seq_lat)
        xs = _outmlp(xs, an, yn, rgs, gab, mod, l, g2, gf, w_tail, seq_lat, True, last)

    return (xp.reshape(x_prompt.shape), xs.reshape(x_sample.shape), new_state)
```
